```python
import jax, jax.numpy as jnp
from jax import lax
import numpy as np

D_MODEL = 1024
BATCH = 8
SEQ = 2048
DEPTH = 4
DEC_BATCH = 32
DEC_SEQ = 2048
PAST_LEN = 128

D_MIX = D_MODEL
D_CONV = D_MIX // 2
CONV_HEAD = 64
N_CONV_HEADS = D_CONV // CONV_HEAD
D_FOURIER = D_MIX - D_CONV
N_FOURIER_GROUPS = 8
FOURIER_GROUP = D_FOURIER // N_FOURIER_GROUPS
CONV_WIDTH = 31
CONV_PAD = CONV_WIDTH // 2
D_IN = 2 * D_CONV + D_FOURIER
D_FF = 2816
RMS_EPS = 1e-6
LN_EPS = 1e-5

kernel_name = "hybrid_conformer_conv_fnet_encoder"


def rms_norm(x, g):
    xf = x.astype(jnp.float32)
    y = xf * lax.rsqrt(jnp.mean(xf * xf, axis=-1, keepdims=True) + RMS_EPS)
    return (y * g.astype(jnp.float32)).astype(x.dtype)


def layer_norm(x, g, b):
    xf = x.astype(jnp.float32)
    mu = jnp.mean(xf, axis=-1, keepdims=True)
    var = jnp.mean(jnp.square(xf - mu), axis=-1, keepdims=True)
    y = (xf - mu) * lax.rsqrt(var + LN_EPS)
    return (y * g.astype(jnp.float32) + b.astype(jnp.float32)).astype(x.dtype)


def swiglu(h, w_gate, w_up, w_down):
    a = jnp.einsum('bsd,df->bsf', h, w_gate)
    u = jnp.einsum('bsd,df->bsf', h, w_up)
    return jnp.einsum('bsf,fd->bsd', jax.nn.silu(a) * u, w_down)


def depthwise_conv(u, w, b):
    c = u.shape[-1]
    out = lax.conv_general_dilated(
        u, w[:, None, :].astype(u.dtype), window_strides=(1,),
        padding=[(CONV_PAD, CONV_PAD)], dimension_numbers=('NWC', 'WIO', 'NWC'),
        feature_group_count=c)
    return out + b


def fourier_mix(u):
    bsz, s, _ = u.shape
    ug = u.reshape(bsz, s, N_FOURIER_GROUPS, FOURIER_GROUP).astype(jnp.float32)
    f = jnp.fft.fft2(ug, axes=(1, 3), norm='ortho').real
    return f.reshape(bsz, s, D_FOURIER).astype(u.dtype)


def trunk(x, ffn1_norm, ffn1_w_gate, ffn1_w_up, ffn1_w_down, mix_norm, w_in,
          conv_w, conv_b, conv_ln_g, conv_ln_b, w_out, ffn2_norm, ffn2_w_gate,
          ffn2_w_up, ffn2_w_down, final_norm):
    for l in range(DEPTH):
        x = x + 0.5 * swiglu(rms_norm(x, ffn1_norm[l]), ffn1_w_gate[l], ffn1_w_up[l], ffn1_w_down[l])
        h = rms_norm(x, mix_norm[l])
        p = jnp.einsum('bsd,dk->bsk', h, w_in[l])
        c_val = p[..., :D_CONV]
        c_gate = p[..., D_CONV:2 * D_CONV]
        f_in = p[..., 2 * D_CONV:]
        c = c_val * jax.nn.sigmoid(c_gate)
        c = depthwise_conv(c, conv_w[l], conv_b[l])
        c = jax.nn.silu(layer_norm(c, conv_ln_g[l], conv_ln_b[l]))
        f = fourier_mix(f_in)
        m = jnp.concatenate([c, f], axis=-1)
        x = x + jnp.einsum('bsk,kd->bsd', m, w_out[l])
        x = x + 0.5 * swiglu(rms_norm(x, ffn2_norm[l]), ffn2_w_gate[l], ffn2_w_up[l], ffn2_w_down[l])
    return rms_norm(x, final_norm)


def setup_inputs(seed: int = 0) -> dict:
    key = jax.random.key(seed)
    ks = jax.random.split(key, 20)
    f32 = jnp.float32

    def w(k, shape, fan_in):
        return jax.random.normal(k, shape, f32) * (fan_in ** -0.5)

    def gain(k, shape):
        return 1.0 + 0.02 * jax.random.normal(k, shape, f32)

    return {
        "x_prompt": jax.random.normal(ks[0], (BATCH, SEQ, D_MODEL), f32),
        "x_sample": jax.random.normal(ks[1], (DEC_BATCH, DEC_SEQ, D_MODEL), f32),
        "ffn1_norm": gain(ks[2], (DEPTH, D_MODEL)),
        "ffn1_w_gate": w(ks[3], (DEPTH, D_MODEL, D_FF), D_MODEL),
        "ffn1_w_up": w(ks[4], (DEPTH, D_MODEL, D_FF), D_MODEL),
        "ffn1_w_down": w(ks[5], (DEPTH, D_FF, D_MODEL), D_FF),
        "mix_norm": gain(ks[6], (DEPTH, D_MODEL)),
        "w_in": w(ks[7], (DEPTH, D_MODEL, D_IN), D_MODEL),
        "conv_w": w(ks[8], (DEPTH, CONV_WIDTH, D_CONV), CONV_WIDTH),
        "conv_b": 0.02 * jax.random.normal(ks[9], (DEPTH, D_CONV), f32),
        "conv_ln_g": gain(ks[10], (DEPTH, D_CONV)),
        "conv_ln_b": 0.02 * jax.random.normal(ks[11], (DEPTH, D_CONV), f32),
        "w_out": w(ks[12], (DEPTH, D_MIX, D_MODEL), D_MIX),
        "ffn2_norm": gain(ks[13], (DEPTH, D_MODEL)),
        "ffn2_w_gate": w(ks[14], (DEPTH, D_MODEL, D_FF), D_MODEL),
        "ffn2_w_up": w(ks[15], (DEPTH, D_MODEL, D_FF), D_MODEL),
        "ffn2_w_down": w(ks[16], (DEPTH, D_FF, D_MODEL), D_FF),
        "final_norm": gain(ks[17], (D_MODEL,)),
    }


def reference(x_prompt, x_sample, ffn1_norm, ffn1_w_gate, ffn1_w_up, ffn1_w_down,
              mix_norm, w_in, conv_w, conv_b, conv_ln_g, conv_ln_b, w_out,
              ffn2_norm, ffn2_w_gate, ffn2_w_up, ffn2_w_down, final_norm):
    y_prompt = trunk(x_prompt, ffn1_norm, ffn1_w_gate, ffn1_w_up, ffn1_w_down, mix_norm,
                     w_in, conv_w, conv_b, conv_ln_g, conv_ln_b, w_out, ffn2_norm,
                     ffn2_w_gate, ffn2_w_up, ffn2_w_down, final_norm)
    y_sample = trunk(x_sample, ffn1_norm, ffn1_w_gate, ffn1_w_up, ffn1_w_down, mix_norm,
                     w_in, conv_w, conv_b, conv_ln_g, conv_ln_b, w_out, ffn2_norm,
                     ffn2_w_gate, ffn2_w_up, ffn2_w_down, final_norm)
    return (y_prompt, y_sample)
```

```python
import functools
import math

import jax
import jax.numpy as jnp
from jax import lax
from jax.experimental import pallas as pl
from jax.experimental.pallas import tpu as pltpu

F32 = jnp.float32
BF16 = jnp.bfloat16

RMS_EPS = 1e-6
LN_EPS = 1e-5
CONV_WIDTH = 31
CONV_PAD = CONV_WIDTH // 2
FOURIER_GROUP = 64

V7X_VMEM_LIMIT_BYTES = 56 * 1024 * 1024
SUBLANES = 8
CONV_HALO = 2 * SUBLANES
CONV_ROWS = 32

FFN_ROWS = 512
INPROJ_ROWS = 512
MIX_ROWS = 512


def _dot(a, b):
    return jnp.dot(a, b, preferred_element_type=F32)


def _rms_norm(x, g):
    return x * lax.rsqrt(jnp.mean(x * x, axis=-1, keepdims=True) + RMS_EPS) * g


def _compiler_params(semantics):
    return pltpu.CompilerParams(dimension_semantics=semantics,
                                vmem_limit_bytes=V7X_VMEM_LIMIT_BYTES)


def _resident(block_shape, index_map):
    return pl.BlockSpec(block_shape, index_map, pipeline_mode=pl.Buffered(1))


def _ffn_kernel(x_ref, g_ref, wg_ref, wu_ref, wd_ref, *rest, final):
    if final:
        gf_ref, o_ref = rest
    else:
        (o_ref,) = rest
    x = x_ref[...]
    h = _rms_norm(x, g_ref[...]).astype(BF16)
    a = _dot(h, wg_ref[...])
    u = _dot(h, wu_ref[...])
    act = (a * jax.nn.sigmoid(a) * u).astype(BF16)
    y = x + 0.5 * _dot(act, wd_ref[...])
    if final:
        y = _rms_norm(y, gf_ref[...])
    o_ref[...] = y


def _ffn(x, g, wg, wu, wd, layer, final_g=None):
    t, d = x.shape
    f = wg.shape[-1]
    rows = FFN_ROWS
    final = final_g is not None
    in_specs = [
        pl.BlockSpec((rows, d), lambda i: (i, 0)),
        _resident((None, 1, d), lambda i: (layer, 0, 0)),
        _resident((None, d, f), lambda i: (layer, 0, 0)),
        _resident((None, d, f), lambda i: (layer, 0, 0)),
        _resident((None, f, d), lambda i: (layer, 0, 0)),
    ]
    args = [x, g, wg, wu, wd]
    if final:
        in_specs.append(_resident((1, d), lambda i: (0, 0)))
        args.append(final_g)
    return pl.pallas_call(
        functools.partial(_ffn_kernel, final=final),
        grid=(t // rows,),
        in_specs=in_specs,
        out_specs=pl.BlockSpec((rows, d), lambda i: (i, 0)),
        out_shape=jax.ShapeDtypeStruct((t, d), F32),
        compiler_params=_compiler_params(("parallel",)),
        name="ffn_final" if final else "ffn",
    )(*args)


def _fold_kernel(win_ref, dft_ref, o_ref):
    n_plain = win_ref.shape[1] - dft_ref.shape[0]
    o_ref[:, :n_plain] = win_ref[:, :n_plain].astype(BF16)
    o_ref[:, n_plain:] = jnp.dot(win_ref[:, n_plain:], dft_ref[...],
                                 preferred_element_type=F32,
                                 precision=lax.Precision.HIGHEST).astype(BF16)


def _fold_w_in(w_in, chan_dft):
    depth, d, d_in = w_in.shape
    fo = chan_dft.shape[0]
    d_out = d_in + fo
    return pl.pallas_call(
        _fold_kernel,
        grid=(depth,),
        in_specs=[pl.BlockSpec((None, d, d_in), lambda l: (l, 0, 0)),
                  _resident((fo, 2 * fo), lambda l: (0, 0))],
        out_specs=pl.BlockSpec((None, d, d_out), lambda l: (l, 0, 0)),
        out_shape=jax.ShapeDtypeStruct((depth, d, d_out), BF16),
        compiler_params=_compiler_params(("parallel",)),
        name="fold_w_in",
    )(w_in, chan_dft)


def _inproj_kernel(x_ref, g_ref, w_ref, glu_ref, ab_ref):
    c = glu_ref.shape[-1]
    fo = ab_ref.shape[-1]
    h = _rms_norm(x_ref[...], g_ref[...]).astype(BF16)
    p = _dot(h, w_ref[...])
    glu_ref[...] = p[:, :c] * jax.nn.sigmoid(p[:, c:2 * c])
    ab_ref[0] = p[:, 2 * c:2 * c + fo].astype(BF16)
    ab_ref[1] = p[:, 2 * c + fo:].astype(BF16)


def _inproj(x, g, w, layer, batch, seq, c, fo):
    t, d = x.shape
    rows = INPROJ_ROWS
    per_seq = seq // rows
    n = w.shape[-1]
    return pl.pallas_call(
        _inproj_kernel,
        grid=(t // rows,),
        in_specs=[pl.BlockSpec((rows, d), lambda i: (i, 0)),
                  _resident((None, 1, d), lambda i: (layer, 0, 0)),
                  _resident((None, d, n), lambda i: (layer, 0, 0))],
        out_specs=[pl.BlockSpec((rows, c), lambda i: (i, 0)),
                   pl.BlockSpec((None, 2, rows, fo),
                                lambda i: (i // per_seq, 0, i % per_seq, 0))],
        out_shape=[jax.ShapeDtypeStruct((t, c), F32),
                   jax.ShapeDtypeStruct((batch, 2, seq, fo), BF16)],
        compiler_params=_compiler_params(("parallel",)),
        name="in_proj",
    )(x, g, w)


def _conv_kernel(glu_ref, w_ref, b_ref, g_ref, beta_ref, o_ref, pad_ref):
    s, c = glu_ref.shape
    zeros = jnp.zeros((CONV_HALO, c), F32)
    pad_ref[0:CONV_HALO, :] = zeros
    pad_ref[CONV_HALO + s:, :] = zeros
    pad_ref[CONV_HALO:CONV_HALO + s, :] = glu_ref[...]
    first = CONV_HALO - CONV_PAD

    def step(i, carry):
        r0 = pl.multiple_of(i * CONV_ROWS, CONV_ROWS)
        win = pad_ref[pl.ds(r0, CONV_ROWS + 2 * CONV_HALO), :]
        acc = jnp.broadcast_to(b_ref[...], (CONV_ROWS, c))
        for k in range(CONV_WIDTH):
            acc = acc + w_ref[k:k + 1, :] * win[first + k:first + k + CONV_ROWS, :]
        mu = jnp.mean(acc, axis=-1, keepdims=True)
        cen = acc - mu
        var = jnp.mean(cen * cen, axis=-1, keepdims=True)
        y = cen * lax.rsqrt(var + LN_EPS) * g_ref[...] + beta_ref[...]
        o_ref[pl.ds(r0, CONV_ROWS), :] = (y * jax.nn.sigmoid(y)).astype(BF16)
        return carry

    lax.fori_loop(0, s // CONV_ROWS, step, 0)


def _conv(glu, w, b, g, beta, layer, batch, seq):
    t, c = glu.shape
    vec = lambda: _resident((None, 1, c), lambda i: (layer, 0, 0))
    return pl.pallas_call(
        _conv_kernel,
        grid=(batch,),
        in_specs=[pl.BlockSpec((seq, c), lambda i: (i, 0)),
                  _resident((None, CONV_WIDTH, c), lambda i: (layer, 0, 0)),
                  vec(), vec(), vec()],
        out_specs=pl.BlockSpec((seq, c), lambda i: (i, 0)),
        out_shape=jax.ShapeDtypeStruct((t, c), BF16),
        scratch_shapes=[pltpu.VMEM((seq + 2 * CONV_HALO, c), F32)],
        compiler_params=_compiler_params(("parallel",)),
        name="conv",
    )(glu, w, b, g, beta)


def _mixout_kernel(x_ref, c_ref, ab_ref, dft_ref, wo_ref, o_ref):
    c = c_ref.shape[-1]
    two, s, fo = ab_ref.shape
    ab = ab_ref[...].reshape(two * s, fo)
    f = _dot(dft_ref[...], ab).astype(BF16)
    o_ref[...] = x_ref[...] + _dot(c_ref[...], wo_ref[:c, :]) + _dot(f, wo_ref[c:, :])


def _mixout(x, conv_out, ab, seq_dft, wo, layer):
    t, d = x.shape
    batch, _, seq, fo = ab.shape
    c = conv_out.shape[-1]
    rows = MIX_ROWS
    per_seq = seq // rows
    return pl.pallas_call(
        _mixout_kernel,
        grid=(batch, per_seq),
        in_specs=[pl.BlockSpec((rows, d), lambda b, j: (b * per_seq + j, 0)),
                  pl.BlockSpec((rows, c), lambda b, j: (b * per_seq + j, 0)),
                  pl.BlockSpec((None, 2, seq, fo), lambda b, j: (b, 0, 0, 0)),
                  pl.BlockSpec((rows, 2 * seq), lambda b, j: (j, 0)),
                  _resident((None, c + fo, d), lambda b, j: (layer, 0, 0))],
        out_specs=pl.BlockSpec((rows, d), lambda b, j: (b * per_seq + j, 0)),
        out_shape=jax.ShapeDtypeStruct((t, d), F32),
        compiler_params=_compiler_params(("parallel", "parallel")),
        name="mix_out",
    )(x, conv_out, ab, seq_dft, wo)


def _dft_tables(n, scale):
    j = lax.iota(jnp.int32, n)
    jk = (j[:, None] * j[None, :]) % n
    ang = jk.astype(F32) * (2.0 * math.pi / n)
    return jnp.cos(ang) * scale, jnp.sin(ang) * scale


def _channel_dft(fo):
    cg, sg = _dft_tables(FOURIER_GROUP, FOURIER_GROUP ** -0.5)
    eye = jnp.eye(fo // FOURIER_GROUP, dtype=F32)
    return jnp.concatenate([jnp.kron(eye, cg), jnp.kron(eye, sg)], axis=1)


def _trunk(x3, w, tables):
    batch, seq, d = x3.shape
    depth = w["w_in_folded"].shape[0]
    c = w["conv_w"].shape[-1]
    fo = w["w_in_folded"].shape[-1] // 2 - c
    x = x3.reshape(batch * seq, d)
    for l in range(depth):
        x = _ffn(x, w["ffn1_norm"], w["ffn1_w_gate"], w["ffn1_w_up"], w["ffn1_w_down"], l)
        glu, ab = _inproj(x, w["mix_norm"], w["w_in_folded"], l, batch, seq, c, fo)
        conv_out = _conv(glu, w["conv_w"], w["conv_b"], w["conv_ln_g"], w["conv_ln_b"],
                         l, batch, seq)
        x = _mixout(x, conv_out, ab, tables, w["w_out"], l)
        x = _ffn(x, w["ffn2_norm"], w["ffn2_w_gate"], w["ffn2_w_up"], w["ffn2_w_down"], l,
                 final_g=w["final_norm"] if l == depth - 1 else None)
    return x.reshape(batch, seq, d)


def kernel(x_prompt, x_sample, ffn1_norm, ffn1_w_gate, ffn1_w_up, ffn1_w_down, mix_norm, w_in, conv_w, conv_b, conv_ln_g, conv_ln_b, w_out, ffn2_norm, ffn2_w_gate, ffn2_w_up, ffn2_w_down, final_norm):
    depth, d, d_in = w_in.shape
    c = conv_w.shape[-1]
    fo = d_in - 2 * c
    seq = x_prompt.shape[1]
    assert x_sample.shape[1] == seq

    row = lambda v: v.reshape(depth, 1, v.shape[-1])
    w = {
        "ffn1_norm": row(ffn1_norm), "ffn2_norm": row(ffn2_norm), "mix_norm": row(mix_norm),
        "ffn1_w_gate": ffn1_w_gate.astype(BF16), "ffn1_w_up": ffn1_w_up.astype(BF16),
        "ffn1_w_down": ffn1_w_down.astype(BF16),
        "ffn2_w_gate": ffn2_w_gate.astype(BF16), "ffn2_w_up": ffn2_w_up.astype(BF16),
        "ffn2_w_down": ffn2_w_down.astype(BF16),
        "w_in_folded": _fold_w_in(w_in, _channel_dft(fo)),
        "conv_w": conv_w, "conv_b": row(conv_b),
        "conv_ln_g": row(conv_ln_g), "conv_ln_b": row(conv_ln_b),
        "w_out": w_out.astype(BF16),
        "final_norm": final_norm.reshape(1, d),
    }
    cos_s, sin_s = _dft_tables(seq, seq ** -0.5)
    tables = jnp.concatenate([cos_s, -sin_s], axis=1).astype(BF16)

    return (_trunk(x_prompt, w, tables), _trunk(x_sample, w, tables))
```

```python
import functools
import math

import jax
import jax.numpy as jnp
from jax import lax
from jax.experimental import pallas as pl
from jax.experimental.pallas import tpu as pltpu

F32 = jnp.float32
BF16 = jnp.bfloat16

RMS_EPS = 1e-6
LN_EPS = 1e-5
CONV_WIDTH = 31
CONV_PAD = CONV_WIDTH // 2
FOURIER_GROUP = 64

V7X_VMEM_LIMIT_BYTES = 56 * 1024 * 1024
SUBLANES = 8
LANES = 128
CONV_HALO = 2 * SUBLANES
CONV_ROWS = 64

FFN_ROWS = 512
INPROJ_ROWS = 512
MIX_ROWS = 512


def _dot(a, b):
    return jnp.dot(a, b, preferred_element_type=F32)


def _rms_norm(x, g):
    return x * lax.rsqrt(jnp.mean(x * x, axis=-1, keepdims=True) + RMS_EPS) * g


def _compiler_params(semantics):
    return pltpu.CompilerParams(dimension_semantics=semantics,
                                vmem_limit_bytes=V7X_VMEM_LIMIT_BYTES)


def _resident(block_shape, index_map):
    return pl.BlockSpec(block_shape, index_map, pipeline_mode=pl.Buffered(1))


def _ffn_kernel(x_ref, g_ref, wg_ref, wu_ref, wd_ref, *rest, final):
    if final:
        gf_ref, o_ref = rest
    else:
        (o_ref,) = rest
    x = x_ref[...]
    h = _rms_norm(x, g_ref[...]).astype(BF16)
    a = _dot(h, wg_ref[...])
    u = _dot(h, wu_ref[...])
    act = (a * jax.nn.sigmoid(a) * u).astype(BF16)
    y = x + 0.5 * _dot(act, wd_ref[...])
    if final:
        y = _rms_norm(y, gf_ref[...])
    o_ref[...] = y


def _ffn(x, g, wg, wu, wd, layer, final_g=None):
    t, d = x.shape
    f = wg.shape[-1]
    rows = FFN_ROWS
    final = final_g is not None
    in_specs = [
        pl.BlockSpec((rows, d), lambda i: (i, 0)),
        _resident((None, 1, d), lambda i: (layer, 0, 0)),
        _resident((None, d, f), lambda i: (layer, 0, 0)),
        _resident((None, d, f), lambda i: (layer, 0, 0)),
        _resident((None, f, d), lambda i: (layer, 0, 0)),
    ]
    args = [x, g, wg, wu, wd]
    if final:
        in_specs.append(_resident((1, d), lambda i: (0, 0)))
        args.append(final_g)
    return pl.pallas_call(
        functools.partial(_ffn_kernel, final=final),
        grid=(t // rows,),
        in_specs=in_specs,
        out_specs=pl.BlockSpec((rows, d), lambda i: (i, 0)),
        out_shape=jax.ShapeDtypeStruct((t, d), F32),
        compiler_params=_compiler_params(("parallel",)),
        name="ffn_final" if final else "ffn",
    )(*args)


def _fold_kernel(win_ref, dft_ref, o_ref):
    n_plain = win_ref.shape[1] - dft_ref.shape[0]
    o_ref[:, :n_plain] = win_ref[:, :n_plain].astype(BF16)
    o_ref[:, n_plain:] = jnp.dot(win_ref[:, n_plain:], dft_ref[...],
                                 preferred_element_type=F32,
                                 precision=lax.Precision.HIGHEST).astype(BF16)


def _fold_w_in(w_in, chan_dft):
    depth, d, d_in = w_in.shape
    fo = chan_dft.shape[0]
    d_out = d_in + fo
    return pl.pallas_call(
        _fold_kernel,
        grid=(depth,),
        in_specs=[pl.BlockSpec((None, d, d_in), lambda l: (l, 0, 0)),
                  _resident((fo, 2 * fo), lambda l: (0, 0))],
        out_specs=pl.BlockSpec((None, d, d_out), lambda l: (l, 0, 0)),
        out_shape=jax.ShapeDtypeStruct((depth, d, d_out), BF16),
        compiler_params=_compiler_params(("parallel",)),
        name="fold_w_in",
    )(w_in, chan_dft)


def _inproj_kernel(x_ref, g_ref, w_ref, glu_ref, ab_ref):
    c = glu_ref.shape[-1]
    fo = ab_ref.shape[-1]
    h = _rms_norm(x_ref[...], g_ref[...]).astype(BF16)
    p = _dot(h, w_ref[...])
    glu_ref[...] = p[:, :c] * jax.nn.sigmoid(p[:, c:2 * c])
    ab_ref[0] = p[:, 2 * c:2 * c + fo].astype(BF16)
    ab_ref[1] = p[:, 2 * c + fo:].astype(BF16)


def _inproj(x, g, w, layer, batch, seq, c, fo):
    t, d = x.shape
    rows = INPROJ_ROWS
    per_seq = seq // rows
    n = w.shape[-1]
    return pl.pallas_call(
        _inproj_kernel,
        grid=(t // rows,),
        in_specs=[pl.BlockSpec((rows, d), lambda i: (i, 0)),
                  _resident((None, 1, d), lambda i: (layer, 0, 0)),
                  _resident((None, d, n), lambda i: (layer, 0, 0))],
        out_specs=[pl.BlockSpec((rows, c), lambda i: (i, 0)),
                   pl.BlockSpec((None, 2, rows, fo),
                                lambda i: (i // per_seq, 0, i % per_seq, 0))],
        out_shape=[jax.ShapeDtypeStruct((t, c), F32),
                   jax.ShapeDtypeStruct((batch, 2, seq, fo), BF16)],
        compiler_params=_compiler_params(("parallel",)),
        name="in_proj",
    )(x, g, w)


def _conv_kernel(glu_ref, w_ref, b_ref, o_ref, ph_ref):
    s, lanes = glu_ref.shape
    zeros = jnp.zeros((CONV_HALO, lanes), F32)
    ph_ref[0, 0:CONV_HALO, :] = zeros
    ph_ref[0, CONV_HALO + s:CONV_HALO + s + CONV_HALO, :] = zeros
    ph_ref[0, CONV_HALO:CONV_HALO + s, :] = glu_ref[...]
    first = CONV_HALO - CONV_PAD

    def step(i, carry):
        r0 = pl.multiple_of(i * CONV_ROWS, CONV_ROWS)
        acc = jnp.broadcast_to(b_ref[...], (CONV_ROWS, lanes))
        for k in range(CONV_WIDTH):
            acc = acc + w_ref[k:k + 1, :] * ph_ref[0, pl.ds(r0 + first + k, CONV_ROWS), :]
        o_ref[pl.ds(r0, CONV_ROWS), :] = acc
        return carry

    lax.fori_loop(0, s // CONV_ROWS, step, 0)


def _conv(glu, w, b, layer, batch, seq):
    t, c = glu.shape
    return pl.pallas_call(
        _conv_kernel,
        grid=(batch, c // LANES),
        in_specs=[pl.BlockSpec((seq, LANES), lambda i, j: (i, j)),
                  pl.BlockSpec((None, CONV_WIDTH, LANES), lambda i, j: (layer, 0, j)),
                  pl.BlockSpec((None, 1, LANES), lambda i, j: (layer, 0, j))],
        out_specs=pl.BlockSpec((seq, LANES), lambda i, j: (i, j)),
        out_shape=jax.ShapeDtypeStruct((t, c), F32),
        scratch_shapes=[pltpu.VMEM((1, seq + 2 * CONV_HALO, LANES), F32)],
        compiler_params=_compiler_params(("parallel", "parallel")),
        name="conv",
    )(glu, w, b)


def _mixout_kernel(x_ref, c_ref, g_ref, beta_ref, ab_ref, dft_ref, wo_ref, o_ref):
    c = c_ref.shape[-1]
    two, s, fo = ab_ref.shape
    cv = c_ref[...]
    cen = cv - jnp.mean(cv, axis=-1, keepdims=True)
    var = jnp.mean(cen * cen, axis=-1, keepdims=True)
    y = cen * lax.rsqrt(var + LN_EPS) * g_ref[...] + beta_ref[...]
    conv_act = (y * jax.nn.sigmoid(y)).astype(BF16)
    ab = ab_ref[...].reshape(two * s, fo)
    f = _dot(dft_ref[...], ab).astype(BF16)
    o_ref[...] = x_ref[...] + _dot(conv_act, wo_ref[:c, :]) + _dot(f, wo_ref[c:, :])


def _mixout(x, conv_out, ln_g, ln_b, ab, seq_dft, wo, layer):
    t, d = x.shape
    batch, _, seq, fo = ab.shape
    c = conv_out.shape[-1]
    rows = MIX_ROWS
    per_seq = seq // rows
    vec = lambda: _resident((None, 1, c), lambda b, j: (layer, 0, 0))
    return pl.pallas_call(
        _mixout_kernel,
        grid=(batch, per_seq),
        in_specs=[pl.BlockSpec((rows, d), lambda b, j: (b * per_seq + j, 0)),
                  pl.BlockSpec((rows, c), lambda b, j: (b * per_seq + j, 0)),
                  vec(), vec(),
                  pl.BlockSpec((None, 2, seq, fo), lambda b, j: (b, 0, 0, 0)),
                  pl.BlockSpec((rows, 2 * seq), lambda b, j: (j, 0)),
                  _resident((None, c + fo, d), lambda b, j: (layer, 0, 0))],
        out_specs=pl.BlockSpec((rows, d), lambda b, j: (b * per_seq + j, 0)),
        out_shape=jax.ShapeDtypeStruct((t, d), F32),
        compiler_params=_compiler_params(("parallel", "parallel")),
        name="mix_out",
    )(x, conv_out, ln_g, ln_b, ab, seq_dft, wo)


def _dft_tables(n, scale):
    j = lax.iota(jnp.int32, n)
    jk = (j[:, None] * j[None, :]) % n
    ang = jk.astype(F32) * (2.0 * math.pi / n)
    return jnp.cos(ang) * scale, jnp.sin(ang) * scale


def _channel_dft(fo):
    cg, sg = _dft_tables(FOURIER_GROUP, FOURIER_GROUP ** -0.5)
    eye = jnp.eye(fo // FOURIER_GROUP, dtype=F32)
    return jnp.concatenate([jnp.kron(eye, cg), jnp.kron(eye, sg)], axis=1)


def _trunk(x3, w, tables):
    batch, seq, d = x3.shape
    depth = w["w_in_folded"].shape[0]
    c = w["conv_w"].shape[-1]
    fo = w["w_in_folded"].shape[-1] // 2 - c
    x = x3.reshape(batch * seq, d)
    for l in range(depth):
        x = _ffn(x, w["ffn1_norm"], w["ffn1_w_gate"], w["ffn1_w_up"], w["ffn1_w_down"], l)
        glu, ab = _inproj(x, w["mix_norm"], w["w_in_folded"], l, batch, seq, c, fo)
        conv_out = _conv(glu, w["conv_w"], w["conv_b"], l, batch, seq)
        x = _mixout(x, conv_out, w["conv_ln_g"], w["conv_ln_b"], ab, tables, w["w_out"], l)
        x = _ffn(x, w["ffn2_norm"], w["ffn2_w_gate"], w["ffn2_w_up"], w["ffn2_w_down"], l,
                 final_g=w["final_norm"] if l == depth - 1 else None)
    return x.reshape(batch, seq, d)


def kernel(x_prompt, x_sample, ffn1_norm, ffn1_w_gate, ffn1_w_up, ffn1_w_down, mix_norm, w_in, conv_w, conv_b, conv_ln_g, conv_ln_b, w_out, ffn2_norm, ffn2_w_gate, ffn2_w_up, ffn2_w_down, final_norm):
    depth, d, d_in = w_in.shape
    c = conv_w.shape[-1]
    fo = d_in - 2 * c
    seq = x_prompt.shape[1]
    assert x_sample.shape[1] == seq

    row = lambda v: v.reshape(depth, 1, v.shape[-1])
    w = {
        "ffn1_norm": row(ffn1_norm), "ffn2_norm": row(ffn2_norm), "mix_norm": row(mix_norm),
        "ffn1_w_gate": ffn1_w_gate.astype(BF16), "ffn1_w_up": ffn1_w_up.astype(BF16),
        "ffn1_w_down": ffn1_w_down.astype(BF16),
        "ffn2_w_gate": ffn2_w_gate.astype(BF16), "ffn2_w_up": ffn2_w_up.astype(BF16),
        "ffn2_w_down": ffn2_w_down.astype(BF16),
        "w_in_folded": _fold_w_in(w_in, _channel_dft(fo)),
        "conv_w": conv_w, "conv_b": row(conv_b),
        "conv_ln_g": row(conv_ln_g), "conv_ln_b": row(conv_ln_b),
        "w_out": w_out.astype(BF16),
        "final_norm": final_norm.reshape(1, d),
    }
    cos_s, sin_s = _dft_tables(seq, seq ** -0.5)
    tables = jnp.concatenate([cos_s, -sin_s], axis=1).astype(BF16)

    return (_trunk(x_prompt, w, tables), _trunk(x_sample, w, tables))
```

```python
import functools
import math

import jax
import jax.numpy as jnp
from jax import lax
from jax.experimental import pallas as pl
from jax.experimental.pallas import tpu as pltpu

F32 = jnp.float32
BF16 = jnp.bfloat16

RMS_EPS = 1e-6
LN_EPS = 1e-5
CONV_WIDTH = 31
CONV_PAD = CONV_WIDTH // 2
FOURIER_GROUP = 64

V7X_VMEM_LIMIT_BYTES = 56 * 1024 * 1024
V7X_MXU_DIM = 256
SUBLANES = 8
LANES = 128
CONV_HALO = 2 * SUBLANES
CONV_ROWS = 64

FFN_ROWS = 512
INPROJ_ROWS = 512
MIX_ROWS = 512


def _dot(a, b):
    return jnp.dot(a, b, preferred_element_type=F32)


def _rms_norm(x, g):
    return x * lax.rsqrt(jnp.mean(x * x, axis=-1, keepdims=True) + RMS_EPS) * g


def _compiler_params(semantics):
    return pltpu.CompilerParams(dimension_semantics=semantics,
                                vmem_limit_bytes=V7X_VMEM_LIMIT_BYTES)


def _resident(block_shape, index_map):
    return pl.BlockSpec(block_shape, index_map, pipeline_mode=pl.Buffered(1))


def _ffn_kernel(x_ref, g_ref, wg_ref, wu_ref, wd_ref, *rest, final):
    if final:
        gf_ref, o_ref = rest
    else:
        (o_ref,) = rest
    x = x_ref[...]
    h = _rms_norm(x, g_ref[...]).astype(BF16)
    a = _dot(h, wg_ref[...])
    u = _dot(h, wu_ref[...])
    act = (a * jax.nn.sigmoid(a) * u).astype(BF16)
    y = x + 0.5 * _dot(act, wd_ref[...])
    if final:
        y = _rms_norm(y, gf_ref[...])
    o_ref[...] = y


def _ffn(x, g, wg, wu, wd, layer, final_g=None):
    t, d = x.shape
    f = wg.shape[-1]
    rows = min(FFN_ROWS, t)
    final = final_g is not None
    in_specs = [
        pl.BlockSpec((rows, d), lambda i: (i, 0)),
        _resident((None, 1, d), lambda i: (layer, 0, 0)),
        _resident((None, d, f), lambda i: (layer, 0, 0)),
        _resident((None, d, f), lambda i: (layer, 0, 0)),
        _resident((None, f, d), lambda i: (layer, 0, 0)),
    ]
    args = [x, g, wg, wu, wd]
    if final:
        in_specs.append(_resident((1, d), lambda i: (0, 0)))
        args.append(final_g)
    return pl.pallas_call(
        functools.partial(_ffn_kernel, final=final),
        grid=(t // rows,),
        in_specs=in_specs,
        out_specs=pl.BlockSpec((rows, d), lambda i: (i, 0)),
        out_shape=jax.ShapeDtypeStruct((t, d), F32),
        compiler_params=_compiler_params(("parallel",)),
        name="ffn_final" if final else "ffn",
    )(*args)


def _fold_kernel(win_ref, dft_ref, o_ref):
    n_plain = win_ref.shape[1] - dft_ref.shape[0]
    o_ref[:, :n_plain] = win_ref[:, :n_plain].astype(BF16)
    o_ref[:, n_plain:] = jnp.dot(win_ref[:, n_plain:], dft_ref[...],
                                 preferred_element_type=F32,
                                 precision=lax.Precision.HIGHEST).astype(BF16)


def _fold_w_in(w_in, chan_dft):
    depth, d, d_in = w_in.shape
    fo = chan_dft.shape[0]
    d_out = d_in + fo
    return pl.pallas_call(
        _fold_kernel,
        grid=(depth,),
        in_specs=[pl.BlockSpec((None, d, d_in), lambda l: (l, 0, 0)),
                  _resident((fo, 2 * fo), lambda l: (0, 0))],
        out_specs=pl.BlockSpec((None, d, d_out), lambda l: (l, 0, 0)),
        out_shape=jax.ShapeDtypeStruct((depth, d, d_out), BF16),
        compiler_params=_compiler_params(("parallel",)),
        name="fold_w_in",
    )(w_in, chan_dft)


def _inproj_kernel(x_ref, g_ref, w_ref, glu_ref, ab_ref):
    c = glu_ref.shape[-1]
    fo = ab_ref.shape[-1]
    h = _rms_norm(x_ref[...], g_ref[...]).astype(BF16)
    p = _dot(h, w_ref[...])
    glu_ref[...] = p[:, :c] * jax.nn.sigmoid(p[:, c:2 * c])
    ab_ref[0] = p[:, 2 * c:2 * c + fo].astype(BF16)
    ab_ref[1] = p[:, 2 * c + fo:].astype(BF16)


def _inproj(x, g, w, layer, batch, seq, c, fo):
    t, d = x.shape
    rows = min(INPROJ_ROWS, seq)
    per_seq = seq // rows
    n = w.shape[-1]
    return pl.pallas_call(
        _inproj_kernel,
        grid=(t // rows,),
        in_specs=[pl.BlockSpec((rows, d), lambda i: (i, 0)),
                  _resident((None, 1, d), lambda i: (layer, 0, 0)),
                  _resident((None, d, n), lambda i: (layer, 0, 0))],
        out_specs=[pl.BlockSpec((rows, c), lambda i: (i, 0)),
                   pl.BlockSpec((None, 2, rows, fo),
                                lambda i: (i // per_seq, 0, i % per_seq, 0))],
        out_shape=[jax.ShapeDtypeStruct((t, c), F32),
                   jax.ShapeDtypeStruct((batch, 2, seq, fo), BF16)],
        compiler_params=_compiler_params(("parallel",)),
        name="in_proj",
    )(x, g, w)


def _seqdft_kernel(ab_ref, cos_ref, sin_ref, rev_ref, o_ref, sh_ref, *, scale):
    _, n, fo = ab_ref.shape
    h = n // 2
    r = rev_ref.shape[0]
    nb = h // r
    rev = rev_ref[...]

    def mirrored(part):
        sh_ref[0:SUBLANES, :] = jnp.zeros((SUBLANES, fo), F32)
        for q in range(nb):
            blk = ab_ref[part, h + (nb - 1 - q) * r:h + (nb - q) * r, :]
            sh_ref[1 + q * r:1 + (q + 1) * r, :] = _dot(rev, blk)
        return sh_ref[0:h, :]

    a_sym = (ab_ref[0, 0:h, :].astype(F32) + mirrored(0)).astype(BF16)
    b_asym = (ab_ref[1, 0:h, :].astype(F32) - mirrored(1)).astype(BF16)
    p_ext = _dot(cos_ref[...], a_sym)
    q = _dot(sin_ref[...], b_asym)
    nyq = ab_ref[0, h:h + 1, :].astype(F32) * scale
    odd = (lax.broadcasted_iota(jnp.int32, (h, fo), 0) & 1) == 1
    p = p_ext[0:h, :] + jnp.where(odd, -nyq, nyq)
    o_ref[0:h, :] = p - q
    o_ref[h:h + 1, :] = p_ext[h:h + 1, :] + nyq
    z = (p + q).astype(BF16)
    for i in range(nb):
        blk = _dot(rev, z[(nb - 1 - i) * r:(nb - i) * r, :])
        keep = r - 1 if i == nb - 1 else r
        o_ref[h + 1 + i * r:h + 1 + i * r + keep, :] = blk[0:keep, :]


def _seqdft(ab, cos_tab, sin_tab, rev):
    batch, _, seq, fo = ab.shape
    h = seq // 2
    r = rev.shape[0]
    return pl.pallas_call(
        functools.partial(_seqdft_kernel, scale=seq ** -0.5),
        grid=(batch,),
        in_specs=[pl.BlockSpec((None, 2, seq, fo), lambda b: (b, 0, 0, 0)),
                  _resident((h + SUBLANES, h), lambda b: (0, 0)),
                  _resident((h, h), lambda b: (0, 0)),
                  _resident((r, r), lambda b: (0, 0))],
        out_specs=pl.BlockSpec((seq, fo), lambda b: (b, 0)),
        out_shape=jax.ShapeDtypeStruct((batch * seq, fo), F32),
        scratch_shapes=[pltpu.VMEM((h + SUBLANES, fo), F32)],
        compiler_params=_compiler_params(("parallel",)),
        name="seq_dft",
    )(ab, cos_tab, sin_tab, rev)


def _mixout_kernel(x_ref, glu_ref, prev_ref, next_ref, cw_ref, cb_ref, g_ref, beta_ref,
                   y_ref, wo_ref, o_ref, pad_ref, conv_ref):
    j = pl.program_id(1)
    rows, c = glu_ref.shape
    head = jnp.where(j > 0, prev_ref[...], 0.0)
    tail = jnp.where(j < pl.num_programs(1) - 1, next_ref[...], 0.0)
    for g in range(c // LANES):
        lanes = slice(g * LANES, (g + 1) * LANES)
        pad_ref[g, 0:CONV_HALO, :] = head[:, lanes]
        pad_ref[g, CONV_HALO:CONV_HALO + rows, :] = glu_ref[:, lanes]
        pad_ref[g, CONV_HALO + rows:, :] = tail[:, lanes]
    first = CONV_HALO - CONV_PAD
    block = min(CONV_ROWS, rows)

    def conv_step(i, carry):
        r0 = pl.multiple_of(i * block, block)
        for g in range(c // LANES):
            lanes = slice(g * LANES, (g + 1) * LANES)
            acc = jnp.broadcast_to(cb_ref[:, lanes], (block, LANES))
            for k in range(CONV_WIDTH):
                acc = acc + cw_ref[k:k + 1, lanes] * pad_ref[g, pl.ds(r0 + first + k, block), :]
            conv_ref[pl.ds(r0, block), lanes] = acc
        return carry

    lax.fori_loop(0, rows // block, conv_step, 0)
    cv = conv_ref[...]
    cen = cv - jnp.mean(cv, axis=-1, keepdims=True)
    var = jnp.mean(cen * cen, axis=-1, keepdims=True)
    ln = cen * lax.rsqrt(var + LN_EPS) * g_ref[...] + beta_ref[...]
    conv_act = (ln * jax.nn.sigmoid(ln)).astype(BF16)
    o_ref[...] = (x_ref[...] + _dot(conv_act, wo_ref[:c, :])
                  + _dot(y_ref[...].astype(BF16), wo_ref[c:, :]))


def _mixout(x, glu, conv_w, conv_b, ln_g, ln_b, y, wo, layer, batch, seq):
    t, d = x.shape
    c = glu.shape[-1]
    fo = y.shape[-1]
    rows = min(MIX_ROWS, seq)
    per_seq = seq // rows
    halo_per_rows = rows // CONV_HALO
    n_halo_blocks = t // CONV_HALO
    vec = lambda: _resident((None, 1, c), lambda b, j: (layer, 0, 0))
    tile = lambda width: pl.BlockSpec((rows, width), lambda b, j: (b * per_seq + j, 0))

    def prev_block(b, j):
        return (jnp.maximum((b * per_seq + j) * halo_per_rows - 1, 0), 0)

    def next_block(b, j):
        return (jnp.minimum((b * per_seq + j + 1) * halo_per_rows, n_halo_blocks - 1), 0)

    return pl.pallas_call(
        _mixout_kernel,
        grid=(batch, per_seq),
        in_specs=[tile(d), tile(c),
                  pl.BlockSpec((CONV_HALO, c), prev_block),
                  pl.BlockSpec((CONV_HALO, c), next_block),
                  _resident((None, CONV_WIDTH, c), lambda b, j: (layer, 0, 0)),
                  vec(), vec(), vec(),
                  tile(fo),
                  _resident((None, c + fo, d), lambda b, j: (layer, 0, 0))],
        out_specs=tile(d),
        out_shape=jax.ShapeDtypeStruct((t, d), F32),
        scratch_shapes=[pltpu.VMEM((c // LANES, rows + 2 * CONV_HALO, LANES), F32),
                        pltpu.VMEM((rows, c), F32)],
        compiler_params=_compiler_params(("parallel", "parallel")),
        name="mix_out",
    )(x, glu, glu, glu, conv_w, conv_b, ln_g, ln_b, y, wo)


def _dft_tables(rows, cols, n, scale):
    j = lax.iota(jnp.int32, rows)
    k = lax.iota(jnp.int32, cols)
    jk = (j[:, None] * k[None, :]) % n
    ang = jk.astype(F32) * (2.0 * math.pi / n)
    return jnp.cos(ang) * scale, jnp.sin(ang) * scale


def _channel_dft(fo):
    cg, sg = _dft_tables(FOURIER_GROUP, FOURIER_GROUP, FOURIER_GROUP, FOURIER_GROUP ** -0.5)
    eye = jnp.eye(fo // FOURIER_GROUP, dtype=F32)
    return jnp.concatenate([jnp.kron(eye, cg), jnp.kron(eye, sg)], axis=1)


def _trunk(x3, w, tables):
    batch, seq, d = x3.shape
    depth = w["w_in_folded"].shape[0]
    c = w["conv_w"].shape[-1]
    fo = w["w_in_folded"].shape[-1] // 2 - c
    x = x3.reshape(batch * seq, d)
    for l in range(depth):
        x = _ffn(x, w["ffn1_norm"], w["ffn1_w_gate"], w["ffn1_w_up"], w["ffn1_w_down"], l)
        glu, ab = _inproj(x, w["mix_norm"], w["w_in_folded"], l, batch, seq, c, fo)
        y = _seqdft(ab, *tables)
        x = _mixout(x, glu, w["conv_w"], w["conv_b"], w["conv_ln_g"], w["conv_ln_b"], y,
                    w["w_out"], l, batch, seq)
        x = _ffn(x, w["ffn2_norm"], w["ffn2_w_gate"], w["ffn2_w_up"], w["ffn2_w_down"], l,
                 final_g=w["final_norm"] if l == depth - 1 else None)
    return x.reshape(batch, seq, d)


def kernel(x_prompt, x_sample, ffn1_norm, ffn1_w_gate, ffn1_w_up, ffn1_w_down, mix_norm, w_in, conv_w, conv_b, conv_ln_g, conv_ln_b, w_out, ffn2_norm, ffn2_w_gate, ffn2_w_up, ffn2_w_down, final_norm):
    depth, d, d_in = w_in.shape
    c = conv_w.shape[-1]
    fo = d_in - 2 * c
    seq = x_prompt.shape[1]
    assert x_sample.shape[1] == seq and seq % 4 == 0

    row = lambda v: v.reshape(depth, 1, v.shape[-1])
    w = {
        "ffn1_norm": row(ffn1_norm), "ffn2_norm": row(ffn2_norm), "mix_norm": row(mix_norm),
        "ffn1_w_gate": ffn1_w_gate.astype(BF16), "ffn1_w_up": ffn1_w_up.astype(BF16),
        "ffn1_w_down": ffn1_w_down.astype(BF16),
        "ffn2_w_gate": ffn2_w_gate.astype(BF16), "ffn2_w_up": ffn2_w_up.astype(BF16),
        "ffn2_w_down": ffn2_w_down.astype(BF16),
        "w_in_folded": _fold_w_in(w_in, _channel_dft(fo)),
        "conv_w": conv_w, "conv_b": row(conv_b),
        "conv_ln_g": row(conv_ln_g), "conv_ln_b": row(conv_ln_b),
        "w_out": w_out.astype(BF16),
        "final_norm": final_norm.reshape(1, d),
    }
    half = seq // 2
    cos_tab, sin_tab = _dft_tables(half + SUBLANES, half, seq, seq ** -0.5)
    rev_rows = min(V7X_MXU_DIM, half)
    rev = (lax.broadcasted_iota(jnp.int32, (rev_rows, rev_rows), 0)
           + lax.broadcasted_iota(jnp.int32, (rev_rows, rev_rows), 1) == rev_rows - 1)
    tables = (cos_tab.astype(BF16), sin_tab[:half].astype(BF16), rev.astype(BF16))

    return (_trunk(x_prompt, w, tables), _trunk(x_sample, w, tables))
```

```python
import functools
import math

import jax
import jax.numpy as jnp
from jax import lax
from jax.experimental import pallas as pl
from jax.experimental.pallas import tpu as pltpu

F32 = jnp.float32
BF16 = jnp.bfloat16

RMS_EPS = 1e-6
LN_EPS = 1e-5
CONV_WIDTH = 31
CONV_PAD = CONV_WIDTH // 2
FOURIER_GROUP = 64

V7X_VMEM_LIMIT_BYTES = 56 * 1024 * 1024
V7X_MXU_DIM = 256
SUBLANES = 8
LANES = 128
CONV_HALO = 2 * SUBLANES
CONV_ROWS = 16
ORDER_TILE_ROWS = 16

FFN_ROWS = 512
INPROJ_ROWS = 512
MIX_ROWS = 512


def _dot(a, b):
    return jnp.dot(a, b, preferred_element_type=F32)


def _rms_norm(x, g):
    return x * lax.rsqrt(jnp.mean(x * x, axis=-1, keepdims=True) + RMS_EPS) * g


def _compiler_params(semantics):
    return pltpu.CompilerParams(dimension_semantics=semantics,
                                vmem_limit_bytes=V7X_VMEM_LIMIT_BYTES)


def _resident(block_shape, index_map):
    return pl.BlockSpec(block_shape, index_map, pipeline_mode=pl.Buffered(1))


def _conv_fill_pad(tile_in_seq, tiles_per_seq, glu_ref, prev_ref, next_ref, pad_ref):
    rows, c = glu_ref.shape
    head = jnp.where(tile_in_seq > 0, prev_ref[...], 0.0)
    tail = jnp.where(tile_in_seq < tiles_per_seq - 1, next_ref[...], 0.0)
    for g in range(c // LANES):
        lanes = slice(g * LANES, (g + 1) * LANES)
        pad_ref[g, 0:CONV_HALO, :] = head[:, lanes]
        pad_ref[g, CONV_HALO:CONV_HALO + rows, :] = glu_ref[:, lanes]
        pad_ref[g, CONV_HALO + rows:, :] = tail[:, lanes]


def _conv_lane_groups(zero, groups, cw_ref, cb_ref, conv_ref, pad_ref):
    rows = conv_ref.shape[0]
    first = CONV_HALO - CONV_PAD
    block = min(CONV_ROWS, rows)
    dep = None
    for g in groups:
        lanes = slice(g * LANES, (g + 1) * LANES)
        for r0 in range(0, rows, block):
            acc = jnp.broadcast_to(cb_ref[:, lanes], (block, LANES))
            for k in range(CONV_WIDTH):
                acc = acc + cw_ref[k:k + 1, lanes] * pad_ref[g, pl.ds(zero + (r0 + first + k), block), :]
            conv_ref[r0:r0 + block, lanes] = acc
            dep = acc if dep is None else dep + acc
    return dep


def _after(zero, value, dep):
    if dep is None:
        return value
    tile = dep[0:ORDER_TILE_ROWS, :].astype(value.dtype)
    reps = (value.shape[0] // tile.shape[0], value.shape[1] // tile.shape[1])
    return jnp.where(zero == 0, value, jnp.tile(tile, reps))


def _ffn_kernel(*refs, final, tiles_per_seq):
    refs = list(refs)
    x_ref, g_ref, wg_ref, wu_ref, wd_ref = refs[:5]
    del refs[:5]
    gf_ref = refs.pop(0) if final else None
    if tiles_per_seq:
        zero_ref, glu_ref, prev_ref, next_ref, cw_ref, cb_ref, o_ref, conv_ref, pad_ref = refs
        zero = zero_ref[0]
        n_groups = glu_ref.shape[-1] // LANES
        _conv_fill_pad(pl.program_id(0) % tiles_per_seq, tiles_per_seq, glu_ref, prev_ref,
                       next_ref, pad_ref)
    else:
        (o_ref,) = refs
    x = x_ref[...]
    h = _rms_norm(x, g_ref[...]).astype(BF16)
    a = _dot(h, wg_ref[...])
    if tiles_per_seq:
        dep = _conv_lane_groups(zero, range(0, n_groups // 2), cw_ref, cb_ref, conv_ref, pad_ref)
        h = _after(zero, h, dep)
    u = _dot(h, wu_ref[...])
    act = (a * jax.nn.sigmoid(a) * u).astype(BF16)
    if tiles_per_seq:
        dep = _conv_lane_groups(zero, range(n_groups // 2, n_groups), cw_ref, cb_ref, conv_ref,
                                pad_ref)
        act = _after(zero, act, dep)
    y = x + 0.5 * _dot(act, wd_ref[...])
    if final:
        y = _rms_norm(y, gf_ref[...])
    o_ref[...] = y


def _ffn(x, row0, nrows, g, wg, wu, wd, layer, final_g=None, conv=None):
    t, d = x.shape
    f = wg.shape[-1]
    rows = min(FFN_ROWS, nrows)
    tile0 = row0 // rows
    final = final_g is not None
    x_tile = pl.BlockSpec((rows, d), lambda i: (tile0 + i, 0))
    in_specs = [
        x_tile,
        _resident((None, 1, d), lambda i: (layer, 0, 0)),
        _resident((None, d, f), lambda i: (layer, 0, 0)),
        _resident((None, d, f), lambda i: (layer, 0, 0)),
        _resident((None, f, d), lambda i: (layer, 0, 0)),
    ]
    args = [x, g, wg, wu, wd]
    if final:
        in_specs.append(_resident((1, d), lambda i: (0, 0)))
        args.append(final_g)
    out_specs = [x_tile]
    out_shape = [jax.ShapeDtypeStruct((t, d), F32)]
    scratch = []
    tiles_per_seq = 0
    if conv is not None:
        glu, conv_w, conv_b, zero, seq = conv
        c = glu.shape[-1]
        assert glu.shape[0] == nrows and seq % rows == 0
        tiles_per_seq = seq // rows
        halo_per_tile = rows // CONV_HALO
        last_halo = nrows // CONV_HALO - 1
        in_specs += [
            pl.BlockSpec(memory_space=pltpu.SMEM),
            pl.BlockSpec((rows, c), lambda i: (i, 0)),
            pl.BlockSpec((CONV_HALO, c), lambda i: (jnp.maximum(i * halo_per_tile - 1, 0), 0)),
            pl.BlockSpec((CONV_HALO, c),
                         lambda i: (jnp.minimum((i + 1) * halo_per_tile, last_halo), 0)),
            _resident((None, CONV_WIDTH, c), lambda i: (layer, 0, 0)),
            _resident((None, 1, c), lambda i: (layer, 0, 0)),
        ]
        args += [zero, glu, glu, glu, conv_w, conv_b]
        out_specs.append(pl.BlockSpec((rows, c), lambda i: (i, 0)))
        out_shape.append(jax.ShapeDtypeStruct((nrows, c), F32))
        scratch.append(pltpu.VMEM((c // LANES, rows + 2 * CONV_HALO, LANES), F32))
    outs = pl.pallas_call(
        functools.partial(_ffn_kernel, final=final, tiles_per_seq=tiles_per_seq),
        grid=(nrows // rows,),
        in_specs=in_specs,
        out_specs=out_specs,
        out_shape=out_shape,
        scratch_shapes=scratch,
        input_output_aliases={0: 0},
        compiler_params=_compiler_params(("parallel",)),
        name=("ffn_final" if final else "ffn") + ("_conv" if conv is not None else ""),
    )(*args)
    return outs if conv is not None else outs[0]


def _fold_kernel(win_ref, dft_ref, o_ref):
    n_plain = win_ref.shape[1] - dft_ref.shape[0]
    o_ref[:, :n_plain] = win_ref[:, :n_plain].astype(BF16)
    o_ref[:, n_plain:] = jnp.dot(win_ref[:, n_plain:], dft_ref[...],
                                 preferred_element_type=F32,
                                 precision=lax.Precision.HIGHEST).astype(BF16)


def _fold_w_in(w_in, chan_dft):
    depth, d, d_in = w_in.shape
    fo = chan_dft.shape[0]
    d_out = d_in + fo
    return pl.pallas_call(
        _fold_kernel,
        grid=(depth,),
        in_specs=[pl.BlockSpec((None, d, d_in), lambda l: (l, 0, 0)),
                  _resident((fo, 2 * fo), lambda l: (0, 0))],
        out_specs=pl.BlockSpec((None, d, d_out), lambda l: (l, 0, 0)),
        out_shape=jax.ShapeDtypeStruct((depth, d, d_out), BF16),
        compiler_params=_compiler_params(("parallel",)),
        name="fold_w_in",
    )(w_in, chan_dft)


def _inproj_kernel(x_ref, g_ref, w_ref, glu_ref, ab_ref):
    c = glu_ref.shape[-1]
    fo = ab_ref.shape[-1]
    h = _rms_norm(x_ref[...], g_ref[...]).astype(BF16)
    p = _dot(h, w_ref[...])
    glu_ref[...] = p[:, :c] * jax.nn.sigmoid(p[:, c:2 * c])
    ab_ref[0] = p[:, 2 * c:2 * c + fo].astype(BF16)
    ab_ref[1] = p[:, 2 * c + fo:].astype(BF16)


def _inproj(x, row0, batch, g, w, layer, seq, c, fo):
    d = x.shape[-1]
    t = batch * seq
    rows = min(INPROJ_ROWS, seq)
    per_seq = seq // rows
    tile0 = row0 // rows
    n = w.shape[-1]
    return pl.pallas_call(
        _inproj_kernel,
        grid=(t // rows,),
        in_specs=[pl.BlockSpec((rows, d), lambda i: (tile0 + i, 0)),
                  _resident((None, 1, d), lambda i: (layer, 0, 0)),
                  _resident((None, d, n), lambda i: (layer, 0, 0))],
        out_specs=[pl.BlockSpec((rows, c), lambda i: (i, 0)),
                   pl.BlockSpec((None, 2, rows, fo),
                                lambda i: (i // per_seq, 0, i % per_seq, 0))],
        out_shape=[jax.ShapeDtypeStruct((t, c), F32),
                   jax.ShapeDtypeStruct((batch, 2, seq, fo), BF16)],
        compiler_params=_compiler_params(("parallel",)),
        name="in_proj",
    )(x, g, w)


def _seqdft_kernel(ab_ref, cos_ref, sin_ref, rev_ref, o_ref, sh_ref, *, scale):
    _, n, fo = ab_ref.shape
    h = n // 2
    r = rev_ref.shape[0]
    nb = h // r
    rev = rev_ref[...]

    def mirrored(part):
        sh_ref[0:SUBLANES, :] = jnp.zeros((SUBLANES, fo), F32)
        for q in range(nb):
            blk = ab_ref[part, h + (nb - 1 - q) * r:h + (nb - q) * r, :]
            sh_ref[1 + q * r:1 + (q + 1) * r, :] = _dot(rev, blk)
        return sh_ref[0:h, :]

    a_sym = (ab_ref[0, 0:h, :].astype(F32) + mirrored(0)).astype(BF16)
    b_asym = (ab_ref[1, 0:h, :].astype(F32) - mirrored(1)).astype(BF16)
    p_ext = _dot(cos_ref[...], a_sym)
    q = _dot(sin_ref[...], b_asym)
    nyq = ab_ref[0, h:h + 1, :].astype(F32) * scale
    odd = (lax.broadcasted_iota(jnp.int32, (h, fo), 0) & 1) == 1
    p = p_ext[0:h, :] + jnp.where(odd, -nyq, nyq)
    o_ref[0:h, :] = p - q
    o_ref[h:h + 1, :] = p_ext[h:h + 1, :] + nyq
    z = (p + q).astype(BF16)
    for i in range(nb):
        blk = _dot(rev, z[(nb - 1 - i) * r:(nb - i) * r, :])
        keep = r - 1 if i == nb - 1 else r
        o_ref[h + 1 + i * r:h + 1 + i * r + keep, :] = blk[0:keep, :]


def _seqdft(ab, cos_tab, sin_tab, rev):
    batch, _, seq, fo = ab.shape
    h = seq // 2
    r = rev.shape[0]
    return pl.pallas_call(
        functools.partial(_seqdft_kernel, scale=seq ** -0.5),
        grid=(batch,),
        in_specs=[pl.BlockSpec((None, 2, seq, fo), lambda b: (b, 0, 0, 0)),
                  _resident((h + SUBLANES, h), lambda b: (0, 0)),
                  _resident((h, h), lambda b: (0, 0)),
                  _resident((r, r), lambda b: (0, 0))],
        out_specs=pl.BlockSpec((seq, fo), lambda b: (b, 0)),
        out_shape=jax.ShapeDtypeStruct((batch * seq, fo), F32),
        scratch_shapes=[pltpu.VMEM((h + SUBLANES, fo), F32)],
        compiler_params=_compiler_params(("parallel",)),
        name="seq_dft",
    )(ab, cos_tab, sin_tab, rev)


def _mixout_kernel(x_ref, conv_ref, g_ref, beta_ref, y_ref, wo_ref, o_ref):
    c = conv_ref.shape[-1]
    cv = conv_ref[...]
    cen = cv - jnp.mean(cv, axis=-1, keepdims=True)
    var = jnp.mean(cen * cen, axis=-1, keepdims=True)
    ln = cen * lax.rsqrt(var + LN_EPS) * g_ref[...] + beta_ref[...]
    conv_act = (ln * jax.nn.sigmoid(ln)).astype(BF16)
    o_ref[...] = (x_ref[...] + _dot(conv_act, wo_ref[:c, :])
                  + _dot(y_ref[...].astype(BF16), wo_ref[c:, :]))


def _mixout(x, row0, conv_out, ln_g, ln_b, y, wo, layer):
    t, d = x.shape
    nrows, c = conv_out.shape
    fo = y.shape[-1]
    rows = min(MIX_ROWS, nrows)
    tile0 = row0 // rows
    x_tile = pl.BlockSpec((rows, d), lambda i: (tile0 + i, 0))
    vec = lambda: _resident((None, 1, c), lambda i: (layer, 0, 0))
    return pl.pallas_call(
        _mixout_kernel,
        grid=(nrows // rows,),
        in_specs=[x_tile,
                  pl.BlockSpec((rows, c), lambda i: (i, 0)),
                  vec(), vec(),
                  pl.BlockSpec((rows, fo), lambda i: (i, 0)),
                  _resident((None, c + fo, d), lambda i: (layer, 0, 0))],
        out_specs=x_tile,
        out_shape=jax.ShapeDtypeStruct((t, d), F32),
        input_output_aliases={0: 0},
        compiler_params=_compiler_params(("parallel",)),
        name="mix_out",
    )(x, conv_out, ln_g, ln_b, y, wo)


def _dft_tables(rows, cols, n, scale):
    j = lax.iota(jnp.int32, rows)
    k = lax.iota(jnp.int32, cols)
    jk = (j[:, None] * k[None, :]) % n
    ang = jk.astype(F32) * (2.0 * math.pi / n)
    return jnp.cos(ang) * scale, jnp.sin(ang) * scale


def _channel_dft(fo):
    cg, sg = _dft_tables(FOURIER_GROUP, FOURIER_GROUP, FOURIER_GROUP, FOURIER_GROUP ** -0.5)
    eye = jnp.eye(fo // FOURIER_GROUP, dtype=F32)
    return jnp.concatenate([jnp.kron(eye, cg), jnp.kron(eye, sg)], axis=1)


def _trunk(x3, w, tables, zero):
    batch, seq, d = x3.shape
    depth = w["w_in_folded"].shape[0]
    c = w["conv_w"].shape[-1]
    fo = w["w_in_folded"].shape[-1] // 2 - c
    assert batch % 2 == 0
    half = batch // 2
    n = half * seq
    x = x3.reshape(batch * seq, d)

    def ffn(which, x, row0, l, **kw):
        p = "ffn%d_" % which
        return _ffn(x, row0, n, w[p + "norm"], w[p + "w_gate"], w[p + "w_up"], w[p + "w_down"], l, **kw)

    def conv_args(glu):
        return (glu, w["conv_w"], w["conv_b"], zero, seq)

    def mix(x, row0, conv_out, ab, l):
        y = _seqdft(ab, *tables)
        return _mixout(x, row0, conv_out, w["conv_ln_g"], w["conv_ln_b"], y, w["w_out"], l)

    for l in range(depth):
        final_g = w["final_norm"] if l == depth - 1 else None
        x = ffn(1, x, 0, l)
        glu_a, ab_a = _inproj(x, 0, half, w["mix_norm"], w["w_in_folded"], l, seq, c, fo)
        x, conv_a = ffn(1, x, n, l, conv=conv_args(glu_a))
        glu_b, ab_b = _inproj(x, n, half, w["mix_norm"], w["w_in_folded"], l, seq, c, fo)
        x = mix(x, 0, conv_a, ab_a, l)
        x, conv_b = ffn(2, x, 0, l, final_g=final_g, conv=conv_args(glu_b))
        x = mix(x, n, conv_b, ab_b, l)
        x = ffn(2, x, n, l, final_g=final_g)
    return x.reshape(batch, seq, d)


def kernel(x_prompt, x_sample, ffn1_norm, ffn1_w_gate, ffn1_w_up, ffn1_w_down, mix_norm, w_in, conv_w, conv_b, conv_ln_g, conv_ln_b, w_out, ffn2_norm, ffn2_w_gate, ffn2_w_up, ffn2_w_down, final_norm):
    depth, d, d_in = w_in.shape
    c = conv_w.shape[-1]
    fo = d_in - 2 * c
    seq = x_prompt.shape[1]
    assert x_sample.shape[1] == seq and seq % 4 == 0

    row = lambda v: v.reshape(depth, 1, v.shape[-1])
    w = {
        "ffn1_norm": row(ffn1_norm), "ffn2_norm": row(ffn2_norm), "mix_norm": row(mix_norm),
        "ffn1_w_gate": ffn1_w_gate.astype(BF16), "ffn1_w_up": ffn1_w_up.astype(BF16),
        "ffn1_w_down": ffn1_w_down.astype(BF16),
        "ffn2_w_gate": ffn2_w_gate.astype(BF16), "ffn2_w_up": ffn2_w_up.astype(BF16),
        "ffn2_w_down": ffn2_w_down.astype(BF16),
        "w_in_folded": _fold_w_in(w_in, _channel_dft(fo)),
        "conv_w": conv_w, "conv_b": row(conv_b),
        "conv_ln_g": row(conv_ln_g), "conv_ln_b": row(conv_ln_b),
        "w_out": w_out.astype(BF16),
        "final_norm": final_norm.reshape(1, d),
    }
    half = seq // 2
    cos_tab, sin_tab = _dft_tables(half + SUBLANES, half, seq, seq ** -0.5)
    rev_rows = min(V7X_MXU_DIM, half)
    rev = (lax.broadcasted_iota(jnp.int32, (rev_rows, rev_rows), 0)
           + lax.broadcasted_iota(jnp.int32, (rev_rows, rev_rows), 1) == rev_rows - 1)
    tables = (cos_tab.astype(BF16), sin_tab[:half].astype(BF16), rev.astype(BF16))

    zero = jnp.zeros((1,), jnp.int32)
    return (_trunk(x_prompt, w, tables, zero), _trunk(x_sample, w, tables, zero))
```

```python
import functools
import math

import jax
import jax.numpy as jnp
from jax import lax
from jax.experimental import pallas as pl
from jax.experimental.pallas import tpu as pltpu

F32 = jnp.float32
BF16 = jnp.bfloat16

RMS_EPS = 1e-6
LN_EPS = 1e-5
CONV_WIDTH = 31
CONV_PAD = CONV_WIDTH // 2
FOURIER_GROUP = 64

V7X_VMEM_LIMIT_BYTES = 56 * 1024 * 1024
V7X_MXU_DIM = 256
SUBLANES = 8
LANES = 128
CONV_HALO = 2 * SUBLANES
CONV_ROWS = 16
ORDER_TILE_ROWS = 16

FFN_ROWS = 512
INPROJ_ROWS = 512


def _dot(a, b):
    return jnp.dot(a, b, preferred_element_type=F32)


def _rms_norm(x, g):
    return x * lax.rsqrt(jnp.mean(x * x, axis=-1, keepdims=True) + RMS_EPS) * g


def _compiler_params(semantics):
    return pltpu.CompilerParams(dimension_semantics=semantics,
                                vmem_limit_bytes=V7X_VMEM_LIMIT_BYTES)


def _resident(block_shape, index_map):
    return pl.BlockSpec(block_shape, index_map, pipeline_mode=pl.Buffered(1))


def _conv_fill_pad(tile_in_seq, tiles_per_seq, glu_ref, prev_ref, next_ref, pad_ref):
    rows, c = glu_ref.shape
    head = jnp.where(tile_in_seq > 0, prev_ref[...], 0.0)
    tail = jnp.where(tile_in_seq < tiles_per_seq - 1, next_ref[...], 0.0)
    for g in range(c // LANES):
        lanes = slice(g * LANES, (g + 1) * LANES)
        pad_ref[g, 0:CONV_HALO, :] = head[:, lanes]
        pad_ref[g, CONV_HALO:CONV_HALO + rows, :] = glu_ref[:, lanes]
        pad_ref[g, CONV_HALO + rows:, :] = tail[:, lanes]


def _conv_lane_groups(zero, groups, cw_ref, cb_ref, conv_ref, pad_ref):
    rows = conv_ref.shape[0]
    first = CONV_HALO - CONV_PAD
    block = min(CONV_ROWS, rows)
    dep = None
    for g in groups:
        lanes = slice(g * LANES, (g + 1) * LANES)
        for r0 in range(0, rows, block):
            acc = jnp.broadcast_to(cb_ref[:, lanes], (block, LANES))
            for k in range(CONV_WIDTH):
                acc = acc + cw_ref[k:k + 1, lanes] * pad_ref[g, pl.ds(zero + (r0 + first + k), block), :]
            conv_ref[r0:r0 + block, lanes] = acc
            dep = acc if dep is None else dep + acc
    return dep


def _after(zero, value, dep):
    if dep is None:
        return value
    tile = dep[0:ORDER_TILE_ROWS, :].astype(value.dtype)
    reps = (value.shape[0] // tile.shape[0], value.shape[1] // tile.shape[1])
    return jnp.where(zero == 0, value, jnp.tile(tile, reps))


def _ffn_kernel(*refs, names, tiles_per_seq):
    r = dict(zip(names, refs))
    host = "glu" in r
    if host:
        zero = r["zero"][0]
        n_groups = r["glu"].shape[-1] // LANES
        conv_args = (r["conv_w"], r["conv_b"], r["conv"], r["pad"])
        _conv_fill_pad(pl.program_id(0) % tiles_per_seq, tiles_per_seq, r["glu"], r["prev"],
                       r["next"], r["pad"])
    x = r["x"][...]
    if "mix_conv" in r:
        c = r["mix_conv"].shape[-1]
        x = x + _dot(r["mix_conv"][...], r["wo"][:c, :]) + _dot(r["mix_y"][...], r["wo"][c:, :])
    h = _rms_norm(x, r["g"][...]).astype(BF16)
    a = _dot(h, r["wg"][...])
    if host:
        h = _after(zero, h, _conv_lane_groups(zero, range(0, n_groups // 2), *conv_args))
    u = _dot(h, r["wu"][...])
    act = (a * jax.nn.sigmoid(a) * u).astype(BF16)
    if host:
        act = _after(zero, act, _conv_lane_groups(zero, range(n_groups // 2, n_groups), *conv_args))
        cv = r["conv"][...]
        cen = cv - jnp.mean(cv, axis=-1, keepdims=True)
        var = jnp.mean(cen * cen, axis=-1, keepdims=True)
        ln = cen * lax.rsqrt(var + LN_EPS) * r["ln_g"][...] + r["ln_b"][...]
        r["conv_act"][...] = (ln * jax.nn.sigmoid(ln)).astype(BF16)
    y = x + 0.5 * _dot(act, r["wd"][...])
    if "final_g" in r:
        y = _rms_norm(y, r["final_g"][...])
    r["o"][...] = y


def _ffn(x_src, x_dst, row0, nrows, g, wg, wu, wd, layer, final_g=None, conv=None, mix=None):
    t, d = x_src.shape
    f = wg.shape[-1]
    rows = min(FFN_ROWS, nrows)
    tile0 = row0 // rows
    x_tile = pl.BlockSpec((rows, d), lambda i: (tile0 + i, 0))
    own_tile = lambda width: pl.BlockSpec((rows, width), lambda i: (i, 0))
    per_layer = lambda *shape: _resident((None,) + shape, lambda i: (layer, 0, 0))
    operands = [("x", x_src, x_tile), ("g", g, per_layer(1, d)), ("wg", wg, per_layer(d, f)),
                ("wu", wu, per_layer(d, f)), ("wd", wd, per_layer(f, d))]
    aliases = {}
    if x_dst is x_src:
        aliases = {0: 0}
    elif x_dst is not None:
        aliases = {len(operands): 0}
        operands.append(("x_dst", x_dst, pl.BlockSpec(memory_space=pl.ANY)))
    if final_g is not None:
        operands.append(("final_g", final_g, _resident((1, d), lambda i: (0, 0))))
    if mix is not None:
        mix_conv, mix_y, wo = mix
        operands += [("mix_conv", mix_conv, own_tile(mix_conv.shape[-1])),
                     ("mix_y", mix_y, own_tile(mix_y.shape[-1])),
                     ("wo", wo, per_layer(*wo.shape[1:]))]
    outputs = [("o", jax.ShapeDtypeStruct((t, d), F32), x_tile)]
    scratch = []
    tiles_per_seq = 0
    if conv is not None:
        glu, conv_w, conv_b, ln_g, ln_b, zero, seq = conv
        c = glu.shape[-1]
        assert glu.shape[0] == nrows and seq % rows == 0
        tiles_per_seq = seq // rows
        halo_per_tile = rows // CONV_HALO
        last_halo = nrows // CONV_HALO - 1
        operands += [
            ("zero", zero, pl.BlockSpec(memory_space=pltpu.SMEM)),
            ("glu", glu, own_tile(c)),
            ("prev", glu, pl.BlockSpec(
                (CONV_HALO, c), lambda i: (jnp.maximum(i * halo_per_tile - 1, 0), 0))),
            ("next", glu, pl.BlockSpec(
                (CONV_HALO, c), lambda i: (jnp.minimum((i + 1) * halo_per_tile, last_halo), 0))),
            ("conv_w", conv_w, per_layer(CONV_WIDTH, c)), ("conv_b", conv_b, per_layer(1, c)),
            ("ln_g", ln_g, per_layer(1, c)), ("ln_b", ln_b, per_layer(1, c)),
        ]
        outputs.append(("conv_act", jax.ShapeDtypeStruct((nrows, c), BF16), own_tile(c)))
        scratch = [("pad", pltpu.VMEM((c // LANES, rows + 2 * CONV_HALO, LANES), F32)),
                   ("conv", pltpu.VMEM((rows, c), F32))]
    names = tuple(n for n, *_ in operands + outputs + scratch)
    outs = pl.pallas_call(
        functools.partial(_ffn_kernel, names=names, tiles_per_seq=tiles_per_seq),
        grid=(nrows // rows,),
        in_specs=[spec for _, _, spec in operands],
        out_specs=[spec for _, _, spec in outputs],
        out_shape=[shape for _, shape, _ in outputs],
        scratch_shapes=[shape for _, shape in scratch],
        input_output_aliases=aliases,
        compiler_params=_compiler_params(("parallel",)),
        name="ffn" + ("_mix" if mix is not None else "") + ("_conv" if conv is not None else "")
        + ("_final" if final_g is not None else ""),
    )(*[arr for _, arr, _ in operands])
    return outs if conv is not None else outs[0]


def _fold_kernel(win_ref, dft_ref, o_ref):
    n_plain = win_ref.shape[1] - dft_ref.shape[0]
    o_ref[:, :n_plain] = win_ref[:, :n_plain].astype(BF16)
    o_ref[:, n_plain:] = jnp.dot(win_ref[:, n_plain:], dft_ref[...],
                                 preferred_element_type=F32,
                                 precision=lax.Precision.HIGHEST).astype(BF16)


def _fold_w_in(w_in, chan_dft):
    depth, d, d_in = w_in.shape
    fo = chan_dft.shape[0]
    d_out = d_in + fo
    return pl.pallas_call(
        _fold_kernel,
        grid=(depth,),
        in_specs=[pl.BlockSpec((None, d, d_in), lambda l: (l, 0, 0)),
                  _resident((fo, 2 * fo), lambda l: (0, 0))],
        out_specs=pl.BlockSpec((None, d, d_out), lambda l: (l, 0, 0)),
        out_shape=jax.ShapeDtypeStruct((depth, d, d_out), BF16),
        compiler_params=_compiler_params(("parallel",)),
        name="fold_w_in",
    )(w_in, chan_dft)


def _inproj_kernel(x_ref, g_ref, w_ref, glu_ref, ab_ref):
    c = glu_ref.shape[-1]
    fo = ab_ref.shape[-1]
    h = _rms_norm(x_ref[...], g_ref[...]).astype(BF16)
    p = _dot(h, w_ref[...])
    glu_ref[...] = p[:, :c] * jax.nn.sigmoid(p[:, c:2 * c])
    ab_ref[0] = p[:, 2 * c:2 * c + fo].astype(BF16)
    ab_ref[1] = p[:, 2 * c + fo:].astype(BF16)


def _inproj(x, row0, batch, g, w, layer, seq, c, fo):
    d = x.shape[-1]
    t = batch * seq
    rows = min(INPROJ_ROWS, seq)
    per_seq = seq // rows
    tile0 = row0 // rows
    n = w.shape[-1]
    return pl.pallas_call(
        _inproj_kernel,
        grid=(t // rows,),
        in_specs=[pl.BlockSpec((rows, d), lambda i: (tile0 + i, 0)),
                  _resident((None, 1, d), lambda i: (layer, 0, 0)),
                  _resident((None, d, n), lambda i: (layer, 0, 0))],
        out_specs=[pl.BlockSpec((rows, c), lambda i: (i, 0)),
                   pl.BlockSpec((None, 2, rows, fo),
                                lambda i: (i // per_seq, 0, i % per_seq, 0))],
        out_shape=[jax.ShapeDtypeStruct((t, c), F32),
                   jax.ShapeDtypeStruct((batch, 2, seq, fo), BF16)],
        compiler_params=_compiler_params(("parallel",)),
        name="in_proj",
    )(x, g, w)


def _seqdft_kernel(ab_ref, cos_ref, sin_ref, rev_ref, o_ref, sh_ref, y_ref, *, scale):
    _, n, fo = ab_ref.shape
    h = n // 2
    r = rev_ref.shape[0]
    nb = h // r
    rev = rev_ref[...]

    def mirrored(part):
        sh_ref[0:SUBLANES, :] = jnp.zeros((SUBLANES, fo), F32)
        for q in range(nb):
            blk = ab_ref[part, h + (nb - 1 - q) * r:h + (nb - q) * r, :]
            sh_ref[1 + q * r:1 + (q + 1) * r, :] = _dot(rev, blk)
        return sh_ref[0:h, :]

    a_sym = (ab_ref[0, 0:h, :].astype(F32) + mirrored(0)).astype(BF16)
    b_asym = (ab_ref[1, 0:h, :].astype(F32) - mirrored(1)).astype(BF16)
    p_ext = _dot(cos_ref[...], a_sym)
    q = _dot(sin_ref[...], b_asym)
    nyq = ab_ref[0, h:h + 1, :].astype(F32) * scale
    odd = (lax.broadcasted_iota(jnp.int32, (h, fo), 0) & 1) == 1
    p = p_ext[0:h, :] + jnp.where(odd, -nyq, nyq)
    y_ref[0:h, :] = p - q
    y_ref[h:h + 1, :] = p_ext[h:h + 1, :] + nyq
    z = (p + q).astype(BF16)
    for i in range(nb):
        blk = _dot(rev, z[(nb - 1 - i) * r:(nb - i) * r, :])
        keep = r - 1 if i == nb - 1 else r
        y_ref[h + 1 + i * r:h + 1 + i * r + keep, :] = blk[0:keep, :]
    o_ref[...] = y_ref[...].astype(BF16)


def _seqdft(ab, cos_tab, sin_tab, rev):
    batch, _, seq, fo = ab.shape
    h = seq // 2
    r = rev.shape[0]
    return pl.pallas_call(
        functools.partial(_seqdft_kernel, scale=seq ** -0.5),
        grid=(batch,),
        in_specs=[pl.BlockSpec((None, 2, seq, fo), lambda b: (b, 0, 0, 0)),
                  _resident((h + SUBLANES, h), lambda b: (0, 0)),
                  _resident((h, h), lambda b: (0, 0)),
                  _resident((r, r), lambda b: (0, 0))],
        out_specs=pl.BlockSpec((seq, fo), lambda b: (b, 0)),
        out_shape=jax.ShapeDtypeStruct((batch * seq, fo), BF16),
        scratch_shapes=[pltpu.VMEM((h + SUBLANES, fo), F32), pltpu.VMEM((seq, fo), F32)],
        compiler_params=_compiler_params(("parallel",)),
        name="seq_dft",
    )(ab, cos_tab, sin_tab, rev)


def _dft_tables(rows, cols, n, scale):
    j = lax.iota(jnp.int32, rows)
    k = lax.iota(jnp.int32, cols)
    jk = (j[:, None] * k[None, :]) % n
    ang = jk.astype(F32) * (2.0 * math.pi / n)
    return jnp.cos(ang) * scale, jnp.sin(ang) * scale


def _channel_dft(fo):
    cg, sg = _dft_tables(FOURIER_GROUP, FOURIER_GROUP, FOURIER_GROUP, FOURIER_GROUP ** -0.5)
    eye = jnp.eye(fo // FOURIER_GROUP, dtype=F32)
    return jnp.concatenate([jnp.kron(eye, cg), jnp.kron(eye, sg)], axis=1)


def _trunk(x3, w, tables, zero):
    batch, seq, d = x3.shape
    depth = w["w_in_folded"].shape[0]
    c = w["conv_w"].shape[-1]
    fo = w["w_in_folded"].shape[-1] // 2 - c
    assert batch % 2 == 0
    half = batch // 2
    n = half * seq
    x_in = x3.reshape(batch * seq, d)

    def ffn(which, src, dst, row0, l, **kw):
        p = "ffn%d_" % which
        return _ffn(src, dst, row0, n, w[p + "norm"], w[p + "w_gate"], w[p + "w_up"],
                    w[p + "w_down"], l, **kw)

    def conv_args(glu):
        return (glu, w["conv_w"], w["conv_b"], w["conv_ln_g"], w["conv_ln_b"], zero, seq)

    def inproj(x, row0, l):
        return _inproj(x, row0, half, w["mix_norm"], w["w_in_folded"], l, seq, c, fo)

    x = None
    for l in range(depth):
        final_g = w["final_norm"] if l == depth - 1 else None
        src = x_in if l == 0 else x
        x = ffn(1, src, x, 0, l)
        glu_a, ab_a = inproj(x, 0, l)
        x, act_a = ffn(1, src, x, n, l, conv=conv_args(glu_a))
        glu_b, ab_b = inproj(x, n, l)
        x, act_b = ffn(2, x, x, 0, l, final_g=final_g, conv=conv_args(glu_b),
                       mix=(act_a, _seqdft(ab_a, *tables), w["w_out"]))
        x = ffn(2, x, x, n, l, final_g=final_g, mix=(act_b, _seqdft(ab_b, *tables), w["w_out"]))
    return x.reshape(batch, seq, d)


def kernel(x_prompt, x_sample, ffn1_norm, ffn1_w_gate, ffn1_w_up, ffn1_w_down, mix_norm, w_in, conv_w, conv_b, conv_ln_g, conv_ln_b, w_out, ffn2_norm, ffn2_w_gate, ffn2_w_up, ffn2_w_down, final_norm):
    depth, d, d_in = w_in.shape
    c = conv_w.shape[-1]
    fo = d_in - 2 * c
    seq = x_prompt.shape[1]
    assert x_sample.shape[1] == seq and seq % 4 == 0

    row = lambda v: v.reshape(depth, 1, v.shape[-1])
    w = {
        "ffn1_norm": row(ffn1_norm), "ffn2_norm": row(ffn2_norm), "mix_norm": row(mix_norm),
        "ffn1_w_gate": ffn1_w_gate.astype(BF16), "ffn1_w_up": ffn1_w_up.astype(BF16),
        "ffn1_w_down": ffn1_w_down.astype(BF16),
        "ffn2_w_gate": ffn2_w_gate.astype(BF16), "ffn2_w_up": ffn2_w_up.astype(BF16),
        "ffn2_w_down": ffn2_w_down.astype(BF16),
        "w_in_folded": _fold_w_in(w_in, _channel_dft(fo)),
        "conv_w": conv_w, "conv_b": row(conv_b),
        "conv_ln_g": row(conv_ln_g), "conv_ln_b": row(conv_ln_b),
        "w_out": w_out.astype(BF16),
        "final_norm": final_norm.reshape(1, d),
    }
    half = seq // 2
    cos_tab, sin_tab = _dft_tables(half + SUBLANES, half, seq, seq ** -0.5)
    rev_rows = min(V7X_MXU_DIM, half)
    rev = (lax.broadcasted_iota(jnp.int32, (rev_rows, rev_rows), 0)
           + lax.broadcasted_iota(jnp.int32, (rev_rows, rev_rows), 1) == rev_rows - 1)
    tables = (cos_tab.astype(BF16), sin_tab[:half].astype(BF16), rev.astype(BF16))

    zero = jnp.zeros((1,), jnp.int32)
    return (_trunk(x_prompt, w, tables, zero), _trunk(x_sample, w, tables, zero))
```

```python
import functools
import math

import jax
import jax.numpy as jnp
from jax import lax
from jax.experimental import pallas as pl
from jax.experimental.pallas import tpu as pltpu

F32 = jnp.float32
BF16 = jnp.bfloat16

RMS_EPS = 1e-6
LN_EPS = 1e-5
CONV_WIDTH = 31
CONV_PAD = CONV_WIDTH // 2
FOURIER_GROUP = 64

V7X_VMEM_LIMIT_BYTES = 56 * 1024 * 1024
V7X_MXU_DIM = 256
SUBLANES = 8
LANES = 128
CONV_HALO = 2 * SUBLANES
CONV_ROWS = 16
ORDER_TILE_ROWS = 16

FFN_ROWS = 512
INPROJ_ROWS = 1024


def _dot(a, b):
    return jnp.dot(a, b, preferred_element_type=F32)


def _rms_norm(x, g):
    return x * lax.rsqrt(jnp.mean(x * x, axis=-1, keepdims=True) + RMS_EPS) * g


def _compiler_params(semantics):
    return pltpu.CompilerParams(dimension_semantics=semantics,
                                vmem_limit_bytes=V7X_VMEM_LIMIT_BYTES)


def _resident(block_shape, index_map):
    return pl.BlockSpec(block_shape, index_map, pipeline_mode=pl.Buffered(1))


def _conv_fill_pad(tile_in_seq, tiles_per_seq, glu_ref, prev_ref, next_ref, pad_ref):
    rows, c = glu_ref.shape
    head = jnp.where(tile_in_seq > 0, prev_ref[...], 0.0)
    tail = jnp.where(tile_in_seq < tiles_per_seq - 1, next_ref[...], 0.0)
    for g in range(c // LANES):
        lanes = slice(g * LANES, (g + 1) * LANES)
        pad_ref[g, 0:CONV_HALO, :] = head[:, lanes]
        pad_ref[g, CONV_HALO:CONV_HALO + rows, :] = glu_ref[:, lanes]
        pad_ref[g, CONV_HALO + rows:, :] = tail[:, lanes]


def _conv_lane_groups(zero, groups, cw_ref, cb_ref, conv_ref, pad_ref):
    rows = conv_ref.shape[0]
    first = CONV_HALO - CONV_PAD
    block = min(CONV_ROWS, rows)
    dep = None
    for g in groups:
        lanes = slice(g * LANES, (g + 1) * LANES)
        for r0 in range(0, rows, block):
            acc = jnp.broadcast_to(cb_ref[:, lanes], (block, LANES))
            for k in range(CONV_WIDTH):
                acc = acc + cw_ref[k:k + 1, lanes] * pad_ref[g, pl.ds(zero + (r0 + first + k), block), :]
            conv_ref[r0:r0 + block, lanes] = acc
            dep = acc if dep is None else dep + acc
    return dep


def _after(zero, value, dep):
    if dep is None:
        return value
    tile = dep[0:ORDER_TILE_ROWS, :].astype(value.dtype)
    reps = (value.shape[0] // tile.shape[0], value.shape[1] // tile.shape[1])
    return jnp.where(zero == 0, value, jnp.tile(tile, reps))


def _ffn_kernel(*refs, names, tiles_per_seq):
    r = dict(zip(names, refs))
    host = "glu" in r
    if host:
        zero = r["zero"][0]
        n_groups = r["glu"].shape[-1] // LANES
        conv_args = (r["conv_w"], r["conv_b"], r["conv"], r["pad"])
        _conv_fill_pad(pl.program_id(0) % tiles_per_seq, tiles_per_seq, r["glu"], r["prev"],
                       r["next"], r["pad"])
    x = r["x"][...]
    if "mix_conv" in r:
        c = r["mix_conv"].shape[-1]
        x = x + _dot(r["mix_conv"][...], r["wo"][:c, :]) + _dot(r["mix_y"][...], r["wo"][c:, :])
    h = _rms_norm(x, r["g"][...]).astype(BF16)
    a = _dot(h, r["wg"][...])
    if host:
        h = _after(zero, h, _conv_lane_groups(zero, range(0, n_groups // 2), *conv_args))
    u = _dot(h, r["wu"][...])
    act = (a * jax.nn.sigmoid(a) * u).astype(BF16)
    if host:
        act = _after(zero, act, _conv_lane_groups(zero, range(n_groups // 2, n_groups), *conv_args))
        cv = r["conv"][...]
        cen = cv - jnp.mean(cv, axis=-1, keepdims=True)
        var = jnp.mean(cen * cen, axis=-1, keepdims=True)
        ln = cen * lax.rsqrt(var + LN_EPS) * r["ln_g"][...] + r["ln_b"][...]
        r["conv_act"][...] = (ln * jax.nn.sigmoid(ln)).astype(BF16)
    y = x + 0.5 * _dot(act, r["wd"][...])
    if "final_g" in r:
        y = _rms_norm(y, r["final_g"][...])
    r["o"][...] = y


def _ffn(x_src, x_dst, row0, nrows, g, wg, wu, wd, layer, final_g=None, conv=None, mix=None):
    t, d = x_src.shape
    f = wg.shape[-1]
    rows = min(FFN_ROWS, nrows)
    tile0 = row0 // rows
    x_tile = pl.BlockSpec((rows, d), lambda i: (tile0 + i, 0))
    own_tile = lambda width: pl.BlockSpec((rows, width), lambda i: (i, 0))
    per_layer = lambda *shape: _resident((None,) + shape, lambda i: (layer, 0, 0))
    operands = [("x", x_src, x_tile), ("g", g, per_layer(1, d)), ("wg", wg, per_layer(d, f)),
                ("wu", wu, per_layer(d, f)), ("wd", wd, per_layer(f, d))]
    aliases = {}
    if x_dst is x_src:
        aliases = {0: 0}
    elif x_dst is not None:
        aliases = {len(operands): 0}
        operands.append(("x_dst", x_dst, pl.BlockSpec(memory_space=pl.ANY)))
    if final_g is not None:
        operands.append(("final_g", final_g, _resident((1, d), lambda i: (0, 0))))
    if mix is not None:
        mix_conv, mix_y, wo = mix
        operands += [("mix_conv", mix_conv, own_tile(mix_conv.shape[-1])),
                     ("mix_y", mix_y, own_tile(mix_y.shape[-1])),
                     ("wo", wo, per_layer(*wo.shape[1:]))]
    outputs = [("o", jax.ShapeDtypeStruct((t, d), F32), x_tile)]
    scratch = []
    tiles_per_seq = 0
    if conv is not None:
        glu, conv_w, conv_b, ln_g, ln_b, zero, seq = conv
        c = glu.shape[-1]
        assert glu.shape[0] == nrows and seq % rows == 0
        tiles_per_seq = seq // rows
        halo_per_tile = rows // CONV_HALO
        last_halo = nrows // CONV_HALO - 1
        operands += [
            ("zero", zero, pl.BlockSpec(memory_space=pltpu.SMEM)),
            ("glu", glu, own_tile(c)),
            ("prev", glu, pl.BlockSpec(
                (CONV_HALO, c), lambda i: (jnp.maximum(i * halo_per_tile - 1, 0), 0))),
            ("next", glu, pl.BlockSpec(
                (CONV_HALO, c), lambda i: (jnp.minimum((i + 1) * halo_per_tile, last_halo), 0))),
            ("conv_w", conv_w, per_layer(CONV_WIDTH, c)), ("conv_b", conv_b, per_layer(1, c)),
            ("ln_g", ln_g, per_layer(1, c)), ("ln_b", ln_b, per_layer(1, c)),
        ]
        outputs.append(("conv_act", jax.ShapeDtypeStruct((nrows, c), BF16), own_tile(c)))
        scratch = [("pad", pltpu.VMEM((c // LANES, rows + 2 * CONV_HALO, LANES), F32)),
                   ("conv", pltpu.VMEM((rows, c), F32))]
    names = tuple(n for n, *_ in operands + outputs + scratch)
    outs = pl.pallas_call(
        functools.partial(_ffn_kernel, names=names, tiles_per_seq=tiles_per_seq),
        grid=(nrows // rows,),
        in_specs=[spec for _, _, spec in operands],
        out_specs=[spec for _, _, spec in outputs],
        out_shape=[shape for _, shape, _ in outputs],
        scratch_shapes=[shape for _, shape in scratch],
        input_output_aliases=aliases,
        compiler_params=_compiler_params(("parallel",)),
        name="ffn" + ("_mix" if mix is not None else "") + ("_conv" if conv is not None else "")
        + ("_final" if final_g is not None else ""),
    )(*[arr for _, arr, _ in operands])
    return outs if conv is not None else outs[0]


def _fold_kernel(win_ref, dft_ref, o_ref):
    n_plain = win_ref.shape[1] - dft_ref.shape[0]
    o_ref[:, :n_plain] = win_ref[:, :n_plain].astype(BF16)
    o_ref[:, n_plain:] = jnp.dot(win_ref[:, n_plain:], dft_ref[...],
                                 preferred_element_type=F32,
                                 precision=lax.Precision.HIGHEST).astype(BF16)


def _fold_w_in(w_in, chan_dft):
    depth, d, d_in = w_in.shape
    fo = chan_dft.shape[0]
    d_out = d_in + fo
    return pl.pallas_call(
        _fold_kernel,
        grid=(depth,),
        in_specs=[pl.BlockSpec((None, d, d_in), lambda l: (l, 0, 0)),
                  _resident((fo, 2 * fo), lambda l: (0, 0))],
        out_specs=pl.BlockSpec((None, d, d_out), lambda l: (l, 0, 0)),
        out_shape=jax.ShapeDtypeStruct((depth, d, d_out), BF16),
        compiler_params=_compiler_params(("parallel",)),
        name="fold_w_in",
    )(w_in, chan_dft)


def _inproj_kernel(x_ref, g_ref, w_ref, glu_ref, ab_ref):
    c = glu_ref.shape[-1]
    fo = ab_ref.shape[-1]
    h = _rms_norm(x_ref[...], g_ref[...]).astype(BF16)
    p = _dot(h, w_ref[...])
    glu_ref[...] = p[:, :c] * jax.nn.sigmoid(p[:, c:2 * c])
    ab_ref[0] = p[:, 2 * c:2 * c + fo].astype(BF16)
    ab_ref[1] = p[:, 2 * c + fo:].astype(BF16)


def _inproj(x, row0, batch, g, w, layer, seq, c, fo):
    d = x.shape[-1]
    t = batch * seq
    rows = min(INPROJ_ROWS, seq)
    per_seq = seq // rows
    tile0 = row0 // rows
    n = w.shape[-1]
    return pl.pallas_call(
        _inproj_kernel,
        grid=(t // rows,),
        in_specs=[pl.BlockSpec((rows, d), lambda i: (tile0 + i, 0)),
                  _resident((None, 1, d), lambda i: (layer, 0, 0)),
                  _resident((None, d, n), lambda i: (layer, 0, 0))],
        out_specs=[pl.BlockSpec((rows, c), lambda i: (i, 0)),
                   pl.BlockSpec((None, 2, rows, fo),
                                lambda i: (i // per_seq, 0, i % per_seq, 0))],
        out_shape=[jax.ShapeDtypeStruct((t, c), F32),
                   jax.ShapeDtypeStruct((batch, 2, seq, fo), BF16)],
        compiler_params=_compiler_params(("parallel",)),
        name="in_proj",
    )(x, g, w)


def _seqdft_kernel(ab_ref, cos_ref, sin_ref, rev_ref, o_ref, sh_ref, y_ref, *, scale):
    _, n, fo = ab_ref.shape
    h = n // 2
    r = rev_ref.shape[0]
    nb = h // r
    rev = rev_ref[...]

    def mirrored(part):
        sh_ref[0:SUBLANES, :] = jnp.zeros((SUBLANES, fo), F32)
        for q in range(nb):
            blk = ab_ref[part, h + (nb - 1 - q) * r:h + (nb - q) * r, :]
            sh_ref[1 + q * r:1 + (q + 1) * r, :] = _dot(rev, blk)
        return sh_ref[0:h, :]

    a_sym = (ab_ref[0, 0:h, :].astype(F32) + mirrored(0)).astype(BF16)
    b_asym = (ab_ref[1, 0:h, :].astype(F32) - mirrored(1)).astype(BF16)
    p_ext = _dot(cos_ref[...], a_sym)
    q = _dot(sin_ref[...], b_asym)
    nyq = ab_ref[0, h:h + 1, :].astype(F32) * scale
    odd = (lax.broadcasted_iota(jnp.int32, (h, fo), 0) & 1) == 1
    p = p_ext[0:h, :] + jnp.where(odd, -nyq, nyq)
    y_ref[0:h, :] = p - q
    y_ref[h:h + 1, :] = p_ext[h:h + 1, :] + nyq
    z = (p + q).astype(BF16)
    for i in range(nb):
        blk = _dot(rev, z[(nb - 1 - i) * r:(nb - i) * r, :])
        keep = r - 1 if i == nb - 1 else r
        y_ref[h + 1 + i * r:h + 1 + i * r + keep, :] = blk[0:keep, :]
    o_ref[...] = y_ref[...].astype(BF16)


def _seqdft(ab, cos_tab, sin_tab, rev):
    batch, _, seq, fo = ab.shape
    h = seq // 2
    r = rev.shape[0]
    return pl.pallas_call(
        functools.partial(_seqdft_kernel, scale=seq ** -0.5),
        grid=(batch,),
        in_specs=[pl.BlockSpec((None, 2, seq, fo), lambda b: (b, 0, 0, 0)),
                  _resident((h + SUBLANES, h), lambda b: (0, 0)),
                  _resident((h, h), lambda b: (0, 0)),
                  _resident((r, r), lambda b: (0, 0))],
        out_specs=pl.BlockSpec((seq, fo), lambda b: (b, 0)),
        out_shape=jax.ShapeDtypeStruct((batch * seq, fo), BF16),
        scratch_shapes=[pltpu.VMEM((h + SUBLANES, fo), F32), pltpu.VMEM((seq, fo), F32)],
        compiler_params=_compiler_params(("parallel",)),
        name="seq_dft",
    )(ab, cos_tab, sin_tab, rev)


def _dft_tables(rows, cols, n, scale):
    j = lax.iota(jnp.int32, rows)
    k = lax.iota(jnp.int32, cols)
    jk = (j[:, None] * k[None, :]) % n
    ang = jk.astype(F32) * (2.0 * math.pi / n)
    return jnp.cos(ang) * scale, jnp.sin(ang) * scale


def _channel_dft(fo):
    cg, sg = _dft_tables(FOURIER_GROUP, FOURIER_GROUP, FOURIER_GROUP, FOURIER_GROUP ** -0.5)
    eye = jnp.eye(fo // FOURIER_GROUP, dtype=F32)
    return jnp.concatenate([jnp.kron(eye, cg), jnp.kron(eye, sg)], axis=1)


def _trunk(x3, w, tables, zero):
    batch, seq, d = x3.shape
    depth = w["w_in_folded"].shape[0]
    c = w["conv_w"].shape[-1]
    fo = w["w_in_folded"].shape[-1] // 2 - c
    assert batch % 2 == 0
    half = batch // 2
    n = half * seq
    x_in = x3.reshape(batch * seq, d)

    def ffn(which, src, dst, row0, l, **kw):
        p = "ffn%d_" % which
        return _ffn(src, dst, row0, n, w[p + "norm"], w[p + "w_gate"], w[p + "w_up"],
                    w[p + "w_down"], l, **kw)

    def conv_args(glu):
        return (glu, w["conv_w"], w["conv_b"], w["conv_ln_g"], w["conv_ln_b"], zero, seq)

    def inproj(x, row0, l):
        return _inproj(x, row0, half, w["mix_norm"], w["w_in_folded"], l, seq, c, fo)

    x = None
    for l in range(depth):
        final_g = w["final_norm"] if l == depth - 1 else None
        x = ffn(1, x_in if l == 0 else x, x, 0, l)
        glu_a, ab_a = inproj(x, 0, l)
        x, act_a = ffn(1, x_in if l == 0 else x, x, n, l, conv=conv_args(glu_a))
        glu_b, ab_b = inproj(x, n, l)
        x, act_b = ffn(2, x, x, 0, l, final_g=final_g, conv=conv_args(glu_b),
                       mix=(act_a, _seqdft(ab_a, *tables), w["w_out"]))
        x = ffn(2, x, x, n, l, final_g=final_g, mix=(act_b, _seqdft(ab_b, *tables), w["w_out"]))
    return x.reshape(batch, seq, d)


def kernel(x_prompt, x_sample, ffn1_norm, ffn1_w_gate, ffn1_w_up, ffn1_w_down, mix_norm, w_in, conv_w, conv_b, conv_ln_g, conv_ln_b, w_out, ffn2_norm, ffn2_w_gate, ffn2_w_up, ffn2_w_down, final_norm):
    depth, d, d_in = w_in.shape
    c = conv_w.shape[-1]
    fo = d_in - 2 * c
    seq = x_prompt.shape[1]
    assert x_sample.shape[1] == seq and seq % 4 == 0

    row = lambda v: v.reshape(depth, 1, v.shape[-1])
    w = {
        "ffn1_norm": row(ffn1_norm), "ffn2_norm": row(ffn2_norm), "mix_norm": row(mix_norm),
        "ffn1_w_gate": ffn1_w_gate.astype(BF16), "ffn1_w_up": ffn1_w_up.astype(BF16),
        "ffn1_w_down": ffn1_w_down.astype(BF16),
        "ffn2_w_gate": ffn2_w_gate.astype(BF16), "ffn2_w_up": ffn2_w_up.astype(BF16),
        "ffn2_w_down": ffn2_w_down.astype(BF16),
        "w_in_folded": _fold_w_in(w_in, _channel_dft(fo)),
        "conv_w": conv_w, "conv_b": row(conv_b),
        "conv_ln_g": row(conv_ln_g), "conv_ln_b": row(conv_ln_b),
        "w_out": w_out.astype(BF16),
        "final_norm": final_norm.reshape(1, d),
    }
    half = seq // 2
    cos_tab, sin_tab = _dft_tables(half + SUBLANES, half, seq, seq ** -0.5)
    rev_rows = min(V7X_MXU_DIM, half)
    rev = (lax.broadcasted_iota(jnp.int32, (rev_rows, rev_rows), 0)
           + lax.broadcasted_iota(jnp.int32, (rev_rows, rev_rows), 1) == rev_rows - 1)
    tables = (cos_tab.astype(BF16), sin_tab[:half].astype(BF16), rev.astype(BF16))

    zero = jnp.zeros((1,), jnp.int32)
    return (_trunk(x_prompt, w, tables, zero), _trunk(x_sample, w, tables, zero))
```

```python
import functools
import math

import jax
import jax.numpy as jnp
from jax import lax
from jax.experimental import pallas as pl
from jax.experimental.pallas import tpu as pltpu

F32 = jnp.float32
BF16 = jnp.bfloat16

RMS_EPS = 1e-6
LN_EPS = 1e-5
CONV_WIDTH = 31
CONV_PAD = CONV_WIDTH // 2
FOURIER_GROUP = 64

V7X_VMEM_LIMIT_BYTES = 56 * 1024 * 1024
V7X_MXU_DIM = 256
SUBLANES = 8
LANES = 128
CONV_HALO = 2 * SUBLANES
CONV_ROWS = 16
ORDER_TILE_ROWS = 16

FFN_ROWS = 512
INPROJ_ROWS = 1024


def _dot(a, b):
    return jnp.dot(a, b, preferred_element_type=F32)


def _rms_norm(x, g):
    return x * lax.rsqrt(jnp.mean(x * x, axis=-1, keepdims=True) + RMS_EPS) * g


def _compiler_params(semantics):
    return pltpu.CompilerParams(dimension_semantics=semantics,
                                vmem_limit_bytes=V7X_VMEM_LIMIT_BYTES)


def _resident(block_shape, index_map):
    return pl.BlockSpec(block_shape, index_map, pipeline_mode=pl.Buffered(1))


def _conv_fill_pad(tile_in_seq, tiles_per_seq, glu_ref, prev_ref, next_ref, pad_ref):
    rows, c = glu_ref.shape
    head = jnp.where(tile_in_seq > 0, prev_ref[...], 0.0)
    tail = jnp.where(tile_in_seq < tiles_per_seq - 1, next_ref[...], 0.0)
    for g in range(c // LANES):
        lanes = slice(g * LANES, (g + 1) * LANES)
        pad_ref[g, 0:CONV_HALO, :] = head[:, lanes]
        pad_ref[g, CONV_HALO:CONV_HALO + rows, :] = glu_ref[:, lanes]
        pad_ref[g, CONV_HALO + rows:, :] = tail[:, lanes]


def _conv_lane_groups(zero, groups, cw_ref, cb_ref, conv_ref, pad_ref):
    rows = conv_ref.shape[0]
    first = CONV_HALO - CONV_PAD
    block = min(CONV_ROWS, rows)
    dep = None
    for g in groups:
        lanes = slice(g * LANES, (g + 1) * LANES)
        for r0 in range(0, rows, block):
            acc = jnp.broadcast_to(cb_ref[:, lanes], (block, LANES))
            for k in range(CONV_WIDTH):
                acc = acc + cw_ref[k:k + 1, lanes] * pad_ref[g, pl.ds(zero + (r0 + first + k), block), :]
            conv_ref[r0:r0 + block, lanes] = acc
            dep = acc if dep is None else dep + acc
    return dep


def _after(zero, value, dep):
    if dep is None:
        return value
    tile = dep[0:ORDER_TILE_ROWS, :].astype(value.dtype)
    reps = (value.shape[0] // tile.shape[0], value.shape[1] // tile.shape[1])
    return jnp.where(zero == 0, value, jnp.tile(tile, reps))


def _ffn_kernel(*refs, names, tiles_per_seq, own_group):
    r = dict(zip(names, refs))
    host = "glu" in r
    if host:
        zero = r["zero"][0]
        n_groups = r["glu"].shape[-1] // LANES
        conv_args = (r["conv_w"], r["conv_b"], r["conv"], r["pad"])
        _conv_fill_pad(pl.program_id(0) % tiles_per_seq, tiles_per_seq, r["glu"], r["prev"],
                       r["next"], r["pad"])
    x = r["x"][...]
    if "mix_conv" in r:
        c = r["mix_conv"].shape[-1]
        x = x + _dot(r["mix_conv"][...], r["wo"][:c, :]) + _dot(r["mix_y"][...], r["wo"][c:, :])
    h = _rms_norm(x, r["g"][...]).astype(BF16)
    a = _dot(h, r["wg"][...])
    if host:
        h = _after(zero, h, _conv_lane_groups(zero, range(0, n_groups // 2), *conv_args))
    u = _dot(h, r["wu"][...])
    act = (a * jax.nn.sigmoid(a) * u).astype(BF16)
    if host:
        act = _after(zero, act, _conv_lane_groups(zero, range(n_groups // 2, n_groups), *conv_args))
        cv = r["conv"][...]
        cen = cv - jnp.mean(cv, axis=-1, keepdims=True)
        var = jnp.mean(cen * cen, axis=-1, keepdims=True)
        ln = cen * lax.rsqrt(var + LN_EPS) * r["ln_g"][...] + r["ln_b"][...]
        r["conv_act"][...] = (ln * jax.nn.sigmoid(ln)).astype(BF16)
    y = x + 0.5 * _dot(act, r["wd"][...])
    if "final_g" in r:
        y = _rms_norm(y, r["final_g"][...])
    if "x_other" in r:
        r["o"][own_group] = y
        r["o"][1 - own_group] = r["x_other"][...]
    else:
        r["o"][...] = y


def _ffn(x, row0, nrows, g, wg, wu, wd, layer, final_g=None, conv=None, mix=None,
         in_place=True):
    t, d = x.shape
    f = wg.shape[-1]
    rows = min(FFN_ROWS, nrows)
    tile0 = row0 // rows
    x_tile = pl.BlockSpec((rows, d), lambda i: (tile0 + i, 0))
    own_tile = lambda width: pl.BlockSpec((rows, width), lambda i: (i, 0))
    per_layer = lambda *shape: _resident((None,) + shape, lambda i: (layer, 0, 0))
    operands = [("x", x, x_tile), ("g", g, per_layer(1, d)), ("wg", wg, per_layer(d, f)),
                ("wu", wu, per_layer(d, f)), ("wd", wd, per_layer(f, d))]
    outputs = [("o", jax.ShapeDtypeStruct((t, d), F32), x_tile)]
    own_group = row0 // nrows
    if not in_place:
        assert t == 2 * nrows and row0 == own_group * nrows
        other_tile0 = (1 - own_group) * (nrows // rows)
        operands.append(
            ("x_other", x, pl.BlockSpec((rows, d), lambda i: (other_tile0 + i, 0))))
        outputs = [("o", jax.ShapeDtypeStruct((2, nrows, d), F32),
                    pl.BlockSpec((2, rows, d), lambda i: (0, i, 0)))]
    if final_g is not None:
        operands.append(("final_g", final_g, _resident((1, d), lambda i: (0, 0))))
    if mix is not None:
        mix_conv, mix_y, wo = mix
        operands += [("mix_conv", mix_conv, own_tile(mix_conv.shape[-1])),
                     ("mix_y", mix_y, own_tile(mix_y.shape[-1])),
                     ("wo", wo, per_layer(*wo.shape[1:]))]
    scratch = []
    tiles_per_seq = 0
    if conv is not None:
        glu, conv_w, conv_b, ln_g, ln_b, zero, seq = conv
        c = glu.shape[-1]
        assert glu.shape[0] == nrows and seq % rows == 0
        tiles_per_seq = seq // rows
        halo_per_tile = rows // CONV_HALO
        last_halo = nrows // CONV_HALO - 1
        operands += [
            ("zero", zero, pl.BlockSpec(memory_space=pltpu.SMEM)),
            ("glu", glu, own_tile(c)),
            ("prev", glu, pl.BlockSpec(
                (CONV_HALO, c), lambda i: (jnp.maximum(i * halo_per_tile - 1, 0), 0))),
            ("next", glu, pl.BlockSpec(
                (CONV_HALO, c), lambda i: (jnp.minimum((i + 1) * halo_per_tile, last_halo), 0))),
            ("conv_w", conv_w, per_layer(CONV_WIDTH, c)), ("conv_b", conv_b, per_layer(1, c)),
            ("ln_g", ln_g, per_layer(1, c)), ("ln_b", ln_b, per_layer(1, c)),
        ]
        outputs.append(("conv_act", jax.ShapeDtypeStruct((nrows, c), BF16), own_tile(c)))
        scratch = [("pad", pltpu.VMEM((c // LANES, rows + 2 * CONV_HALO, LANES), F32)),
                   ("conv", pltpu.VMEM((rows, c), F32))]
    names = tuple(n for n, *_ in operands + outputs + scratch)
    outs = pl.pallas_call(
        functools.partial(_ffn_kernel, names=names, tiles_per_seq=tiles_per_seq,
                          own_group=own_group),
        grid=(nrows // rows,),
        in_specs=[spec for _, _, spec in operands],
        out_specs=[spec for _, _, spec in outputs],
        out_shape=[shape for _, shape, _ in outputs],
        scratch_shapes=[shape for _, shape in scratch],
        input_output_aliases={0: 0} if in_place else {},
        compiler_params=_compiler_params(("parallel",)),
        name="ffn" + ("_mix" if mix is not None else "") + ("_conv" if conv is not None else "")
        + ("_final" if final_g is not None else ""),
    )(*[arr for _, arr, _ in operands])
    outs = [outs[0].reshape(t, d), *outs[1:]]
    return outs if conv is not None else outs[0]


def _fold_kernel(win_ref, dft_ref, o_ref):
    n_plain = win_ref.shape[1] - dft_ref.shape[0]
    o_ref[:, :n_plain] = win_ref[:, :n_plain].astype(BF16)
    o_ref[:, n_plain:] = jnp.dot(win_ref[:, n_plain:], dft_ref[...],
                                 preferred_element_type=F32,
                                 precision=lax.Precision.HIGHEST).astype(BF16)


def _fold_w_in(w_in, chan_dft):
    depth, d, d_in = w_in.shape
    fo = chan_dft.shape[0]
    d_out = d_in + fo
    return pl.pallas_call(
        _fold_kernel,
        grid=(depth,),
        in_specs=[pl.BlockSpec((None, d, d_in), lambda l: (l, 0, 0)),
                  _resident((fo, 2 * fo), lambda l: (0, 0))],
        out_specs=pl.BlockSpec((None, d, d_out), lambda l: (l, 0, 0)),
        out_shape=jax.ShapeDtypeStruct((depth, d, d_out), BF16),
        compiler_params=_compiler_params(("parallel",)),
        name="fold_w_in",
    )(w_in, chan_dft)


def _inproj_kernel(x_ref, g_ref, w_ref, glu_ref, ab_ref):
    c = glu_ref.shape[-1]
    fo = ab_ref.shape[-1]
    h = _rms_norm(x_ref[...], g_ref[...]).astype(BF16)
    p = _dot(h, w_ref[...])
    glu_ref[...] = p[:, :c] * jax.nn.sigmoid(p[:, c:2 * c])
    ab_ref[0] = p[:, 2 * c:2 * c + fo].astype(BF16)
    ab_ref[1] = p[:, 2 * c + fo:].astype(BF16)


def _inproj(x, row0, batch, g, w, layer, seq, c, fo):
    d = x.shape[-1]
    t = batch * seq
    rows = min(INPROJ_ROWS, seq)
    per_seq = seq // rows
    tile0 = row0 // rows
    n = w.shape[-1]
    return pl.pallas_call(
        _inproj_kernel,
        grid=(t // rows,),
        in_specs=[pl.BlockSpec((rows, d), lambda i: (tile0 + i, 0)),
                  _resident((None, 1, d), lambda i: (layer, 0, 0)),
                  _resident((None, d, n), lambda i: (layer, 0, 0))],
        out_specs=[pl.BlockSpec((rows, c), lambda i: (i, 0)),
                   pl.BlockSpec((None, 2, rows, fo),
                                lambda i: (i // per_seq, 0, i % per_seq, 0))],
        out_shape=[jax.ShapeDtypeStruct((t, c), F32),
                   jax.ShapeDtypeStruct((batch, 2, seq, fo), BF16)],
        compiler_params=_compiler_params(("parallel",)),
        name="in_proj",
    )(x, g, w)


def _seqdft_kernel(ab_ref, cos_ref, sin_ref, rev_ref, o_ref, sh_ref, y_ref, *, scale):
    _, n, fo = ab_ref.shape
    h = n // 2
    r = rev_ref.shape[0]
    nb = h // r
    rev = rev_ref[...]

    def mirrored(part):
        sh_ref[0:SUBLANES, :] = jnp.zeros((SUBLANES, fo), F32)
        for q in range(nb):
            blk = ab_ref[part, h + (nb - 1 - q) * r:h + (nb - q) * r, :]
            sh_ref[1 + q * r:1 + (q + 1) * r, :] = _dot(rev, blk)
        return sh_ref[0:h, :]

    a_sym = (ab_ref[0, 0:h, :].astype(F32) + mirrored(0)).astype(BF16)
    b_asym = (ab_ref[1, 0:h, :].astype(F32) - mirrored(1)).astype(BF16)
    p_ext = _dot(cos_ref[...], a_sym)
    q = _dot(sin_ref[...], b_asym)
    nyq = ab_ref[0, h:h + 1, :].astype(F32) * scale
    odd = (lax.broadcasted_iota(jnp.int32, (h, fo), 0) & 1) == 1
    p = p_ext[0:h, :] + jnp.where(odd, -nyq, nyq)
    y_ref[0:h, :] = p - q
    y_ref[h:h + 1, :] = p_ext[h:h + 1, :] + nyq
    z = (p + q).astype(BF16)
    for i in range(nb):
        blk = _dot(rev, z[(nb - 1 - i) * r:(nb - i) * r, :])
        keep = r - 1 if i == nb - 1 else r
        y_ref[h + 1 + i * r:h + 1 + i * r + keep, :] = blk[0:keep, :]
    o_ref[...] = y_ref[...].astype(BF16)


def _seqdft(ab, cos_tab, sin_tab, rev):
    batch, _, seq, fo = ab.shape
    h = seq // 2
    r = rev.shape[0]
    return pl.pallas_call(
        functools.partial(_seqdft_kernel, scale=seq ** -0.5),
        grid=(batch,),
        in_specs=[pl.BlockSpec((None, 2, seq, fo), lambda b: (b, 0, 0, 0)),
                  _resident((h + SUBLANES, h), lambda b: (0, 0)),
                  _resident((h, h), lambda b: (0, 0)),
                  _resident((r, r), lambda b: (0, 0))],
        out_specs=pl.BlockSpec((seq, fo), lambda b: (b, 0)),
        out_shape=jax.ShapeDtypeStruct((batch * seq, fo), BF16),
        scratch_shapes=[pltpu.VMEM((h + SUBLANES, fo), F32), pltpu.VMEM((seq, fo), F32)],
        compiler_params=_compiler_params(("parallel",)),
        name="seq_dft",
    )(ab, cos_tab, sin_tab, rev)


def _dft_tables(rows, cols, n, scale):
    j = lax.iota(jnp.int32, rows)
    k = lax.iota(jnp.int32, cols)
    jk = (j[:, None] * k[None, :]) % n
    ang = jk.astype(F32) * (2.0 * math.pi / n)
    return jnp.cos(ang) * scale, jnp.sin(ang) * scale


def _channel_dft(fo):
    cg, sg = _dft_tables(FOURIER_GROUP, FOURIER_GROUP, FOURIER_GROUP, FOURIER_GROUP ** -0.5)
    eye = jnp.eye(fo // FOURIER_GROUP, dtype=F32)
    return jnp.concatenate([jnp.kron(eye, cg), jnp.kron(eye, sg)], axis=1)


def _trunk(x3, w, tables, zero):
    batch, seq, d = x3.shape
    depth = w["w_in_folded"].shape[0]
    c = w["conv_w"].shape[-1]
    fo = w["w_in_folded"].shape[-1] // 2 - c
    assert batch % 2 == 0
    half = batch // 2
    n = half * seq
    x = x3.reshape(batch * seq, d)

    def ffn(which, x, row0, l, **kw):
        p = "ffn%d_" % which
        return _ffn(x, row0, n, w[p + "norm"], w[p + "w_gate"], w[p + "w_up"],
                    w[p + "w_down"], l, **kw)

    def conv_args(glu):
        return (glu, w["conv_w"], w["conv_b"], w["conv_ln_g"], w["conv_ln_b"], zero, seq)

    def inproj(x, row0, l):
        return _inproj(x, row0, half, w["mix_norm"], w["w_in_folded"], l, seq, c, fo)

    for l in range(depth):
        final_g = w["final_norm"] if l == depth - 1 else None
        x = ffn(1, x, 0, l, in_place=l > 0)
        glu_a, ab_a = inproj(x, 0, l)
        x, act_a = ffn(1, x, n, l, conv=conv_args(glu_a))
        glu_b, ab_b = inproj(x, n, l)
        x, act_b = ffn(2, x, 0, l, final_g=final_g, conv=conv_args(glu_b),
                       mix=(act_a, _seqdft(ab_a, *tables), w["w_out"]))
        x = ffn(2, x, n, l, final_g=final_g, mix=(act_b, _seqdft(ab_b, *tables), w["w_out"]))
    return x.reshape(batch, seq, d)


def kernel(x_prompt, x_sample, ffn1_norm, ffn1_w_gate, ffn1_w_up, ffn1_w_down, mix_norm, w_in, conv_w, conv_b, conv_ln_g, conv_ln_b, w_out, ffn2_norm, ffn2_w_gate, ffn2_w_up, ffn2_w_down, final_norm):
    depth, d, d_in = w_in.shape
    c = conv_w.shape[-1]
    fo = d_in - 2 * c
    seq = x_prompt.shape[1]
    assert x_sample.shape[1] == seq and seq % 4 == 0

    row = lambda v: v.reshape(depth, 1, v.shape[-1])
    w = {
        "ffn1_norm": row(ffn1_norm), "ffn2_norm": row(ffn2_norm), "mix_norm": row(mix_norm),
        "ffn1_w_gate": ffn1_w_gate.astype(BF16), "ffn1_w_up": ffn1_w_up.astype(BF16),
        "ffn1_w_down": ffn1_w_down.astype(BF16),
        "ffn2_w_gate": ffn2_w_gate.astype(BF16), "ffn2_w_up": ffn2_w_up.astype(BF16),
        "ffn2_w_down": ffn2_w_down.astype(BF16),
        "w_in_folded": _fold_w_in(w_in, _channel_dft(fo)),
        "conv_w": conv_w, "conv_b": row(conv_b),
        "conv_ln_g": row(conv_ln_g), "conv_ln_b": row(conv_ln_b),
        "w_out": w_out.astype(BF16),
        "final_norm": final_norm.reshape(1, d),
    }
    half = seq // 2
    cos_tab, sin_tab = _dft_tables(half + SUBLANES, half, seq, seq ** -0.5)
    rev_rows = min(V7X_MXU_DIM, half)
    rev = (lax.broadcasted_iota(jnp.int32, (rev_rows, rev_rows), 0)
           + lax.broadcasted_iota(jnp.int32, (rev_rows, rev_rows), 1) == rev_rows - 1)
    tables = (cos_tab.astype(BF16), sin_tab[:half].astype(BF16), rev.astype(BF16))

    zero = jnp.zeros((1,), jnp.int32)
    return (_trunk(x_prompt, w, tables, zero), _trunk(x_sample, w, tables, zero))
```
